```python
import jax, jax.numpy as jnp
from jax import lax
import numpy as np

D_MODEL = 1024
BATCH = 8
SEQ = 2048
DEPTH = 1
DEC_BATCH = 128
DEC_SEQ = 1
PAST_LEN = 16384
PAGE_SIZE = 128

D_MIX = D_MODEL
D_GMLP = D_MIX // 2
GMLP_HEADS = 4
GMLP_HD = D_GMLP // GMLP_HEADS
CHUNK = 128
D_SSD = D_MIX - D_GMLP
SSD_HEAD_DIM = 64
SSD_HEADS = D_SSD // SSD_HEAD_DIM
SSD_GROUPS = 2
HEADS_PER_GROUP = SSD_HEADS // SSD_GROUPS
D_STATE = 128
CONV_W = 4
CONV_DIM = D_SSD + 2 * SSD_GROUPS * D_STATE
SSD_CHUNK = 128
N_MEM = 256
X_HEADS = 4
X_HD = D_MODEL // X_HEADS
D_FF = -(-8 * D_MODEL // (3 * 256)) * 256
D_IN = 2 * D_GMLP + D_SSD + CONV_DIM + SSD_HEADS
SPLITS = [D_GMLP, 2 * D_GMLP, 2 * D_GMLP + D_SSD, 2 * D_GMLP + D_SSD + CONV_DIM]
EPS = 1e-6

kernel_name = "hymba_gmlp_ssd_memxattn_step"

f32 = jnp.float32


def rmsnorm(x, g):
    xf = x.astype(f32)
    y = xf * lax.rsqrt(jnp.mean(xf * xf, -1, keepdims=True) + EPS)
    return (y * g.astype(f32)).astype(x.dtype)


def layernorm(x, g, b):
    xf = x.astype(f32)
    mu = jnp.mean(xf, -1, keepdims=True)
    xc = xf - mu
    y = xc * lax.rsqrt(jnp.mean(xc * xc, -1, keepdims=True) + EPS)
    return (y * g.astype(f32) + b.astype(f32)).astype(x.dtype)


def gated_group_norm(y, z, g):
    yz = y.astype(f32) * jax.nn.silu(z.astype(f32))
    shp = yz.shape
    yz = yz.reshape(*shp[:-1], SSD_GROUPS, D_SSD // SSD_GROUPS)
    yz = yz * lax.rsqrt(jnp.mean(yz * yz, -1, keepdims=True) + EPS)
    return yz.reshape(shp) * g.astype(f32)


def gmlp_mix(u, v, ws, bs):
    b, L, _ = v.shape
    T = min(L, CHUNK)
    nc = L // T
    mask = jnp.tril(jnp.ones((T, T), dtype=bool))
    w = jnp.where(mask, ws[:, :T, :T], jnp.zeros((), ws.dtype))
    vc = v.reshape(b, nc, T, GMLP_HEADS, GMLP_HD)
    mixed = jnp.einsum('hts,bcshd->bcthd', w, vc) + jnp.swapaxes(bs[:, :T], 0, 1)[:, :, None]
    return u * mixed.reshape(b, L, D_GMLP)


def causal_conv(xpad, w, bias, L):
    out = bias
    for k in range(CONV_W):
        out = out + xpad[:, k:k + L] * w[k]
    return jax.nn.silu(out)


def ssd_chunked(x, dt, A, Bm, Cm):
    b, L = x.shape[:2]
    Q = SSD_CHUNK
    nc = L // Q
    G, J, P, N = SSD_GROUPS, HEADS_PER_GROUP, SSD_HEAD_DIM, D_STATE
    xg = x.reshape(b, nc, Q, G, J, P)
    dtg = dt.reshape(b, nc, Q, G, J)
    Bc = Bm.reshape(b, nc, Q, G, N)
    Cc = Cm.reshape(b, nc, Q, G, N)
    a_cum = jnp.cumsum(dtg * A.reshape(G, J), axis=2)
    seg = a_cum[:, :, :, None] - a_cum[:, :, None, :]
    causal = jnp.tril(jnp.ones((Q, Q), dtype=bool))[:, :, None, None]
    decay = jnp.exp(jnp.where(causal, seg, -jnp.inf))
    CB = jnp.einsum('bctgn,bcsgn->bctsg', Cc, Bc)
    y_diag = jnp.einsum('bctsg,bctsgj,bcsgj,bcsgjp->bctgjp', CB, decay, dtg, xg)
    decay_end = jnp.exp(a_cum[:, :, -1:] - a_cum)
    states = jnp.einsum('bcsgn,bcsgj,bcsgjp->bcgjpn', Bc, decay_end * dtg, xg)
    chunk_decay = jnp.exp(a_cum[:, :, -1])

    def step(h, inp):
        dec, st = inp
        return h * dec[..., None, None] + st, h

    h0 = jnp.zeros((b, G, J, P, N), f32)
    h_final, h_prev = lax.scan(step, h0, (jnp.swapaxes(chunk_decay, 0, 1), jnp.swapaxes(states, 0, 1)))
    h_prev = jnp.swapaxes(h_prev, 0, 1)
    y_off = jnp.einsum('bctgn,bctgj,bcgjpn->bctgjp', Cc, jnp.exp(a_cum), h_prev)
    y = (y_diag + y_off).reshape(b, L, SSD_HEADS, P)
    return y, h_final.reshape(b, SSD_HEADS, P, N)


def ssd_recurrent(x, dt, A, Bm, Cm, h0):
    b, L = x.shape[:2]
    G, J, P, N = SSD_GROUPS, HEADS_PER_GROUP, SSD_HEAD_DIM, D_STATE
    Ag = A.reshape(G, J)
    xs = jnp.moveaxis(x.reshape(b, L, G, J, P), 1, 0)
    dts = jnp.moveaxis(dt.reshape(b, L, G, J), 1, 0)
    Bs = jnp.moveaxis(Bm, 1, 0)
    Cs = jnp.moveaxis(Cm, 1, 0)

    def step(h, inp):
        x_t, dt_t, B_t, C_t = inp
        h = h * jnp.exp(dt_t * Ag)[..., None, None] + jnp.einsum('bgj,bgjp,bgn->bgjpn', dt_t, x_t, B_t)
        y = jnp.einsum('bgn,bgjpn->bgjp', C_t, h)
        return h, y

    h, ys = lax.scan(step, h0.reshape(b, G, J, P, N), (xs, dts, Bs, Cs))
    y = jnp.moveaxis(ys, 0, 1).reshape(b, L, SSD_HEADS, P)
    return y, h.reshape(b, SSD_HEADS, P, N)


def mixer_block(hn, conv_prefix, h0, w_in, ln_g, ln_b, ws, bs, cw, cb, dtb, alog, dsk, sn, wo):
    b, L, _ = hn.shape
    proj = hn @ w_in
    u, v, z, xbc, dt_raw = jnp.split(proj, SPLITS, axis=-1)
    u = jax.nn.gelu(u)
    v = layernorm(jax.nn.gelu(v), ln_g, ln_b)
    g_out = gmlp_mix(u, v, ws, bs)
    xpad = jnp.concatenate([conv_prefix.astype(xbc.dtype), xbc], axis=1)
    xbc_c = causal_conv(xpad, cw, cb, L)
    xs, Bm, Cm = jnp.split(xbc_c, [D_SSD, D_SSD + SSD_GROUPS * D_STATE], axis=-1)
    xs = xs.reshape(b, L, SSD_HEADS, SSD_HEAD_DIM).astype(f32)
    Bm = Bm.reshape(b, L, SSD_GROUPS, D_STATE).astype(f32)
    Cm = Cm.reshape(b, L, SSD_GROUPS, D_STATE).astype(f32)
    dt = jax.nn.softplus(dt_raw.astype(f32) + dtb.astype(f32))
    A = -jnp.exp(alog.astype(f32))
    if h0 is None:
        y, h = ssd_chunked(xs, dt, A, Bm, Cm)
    else:
        y, h = ssd_recurrent(xs, dt, A, Bm, Cm, h0.astype(f32))
    y = y + dsk.astype(f32)[:, None] * xs
    s_out = gated_group_norm(y.reshape(b, L, D_SSD), z, sn).astype(hn.dtype)
    out = jnp.concatenate([g_out, s_out], axis=-1) @ wo
    return out, xpad[:, -(CONV_W - 1):], h.astype(hn.dtype), v


def mem_kv(mem, g, w_k, w_v):
    mn = rmsnorm(mem, g)
    b, M, _ = mem.shape
    return (mn @ w_k).reshape(b, M, X_HEADS, X_HD), (mn @ w_v).reshape(b, M, X_HEADS, X_HD)


def cross_attn(hn, k, v, w_q, w_xo):
    b, L, _ = hn.shape
    q = (hn @ w_q).reshape(b, L, X_HEADS, X_HD)
    s = jnp.einsum('bthd,bmhd->bhtm', q.astype(f32), k.astype(f32)) * (X_HD ** -0.5)
    p = jax.nn.softmax(s, axis=-1)
    o = jnp.einsum('bhtm,bmhd->bthd', p, v.astype(f32)).astype(hn.dtype)
    return o.reshape(b, L, D_MODEL) @ w_xo


def swiglu(hn, w_gate, w_up, w_down):
    return (jax.nn.silu(hn @ w_gate) * (hn @ w_up)) @ w_down


def setup_inputs(seed: int = 0) -> dict:
    key = jax.random.key(seed)
    ks = iter(jax.random.split(key, 48))

    def nrm(shape, scale):
        return jax.random.normal(next(ks), shape, f32) * scale

    def gain(n):
        return 1.0 + nrm((DEPTH, n), 0.02)

    dt0 = jnp.exp(jax.random.uniform(next(ks), (DEPTH, SSD_HEADS), f32, np.log(1e-3), np.log(1e-1)))
    dt_bias = dt0 + jnp.log(-jnp.expm1(-dt0))
    a_log = jnp.log(jax.random.uniform(next(ks), (DEPTH, SSD_HEADS), f32, 1.0, 16.0))
    return {
        "x_prompt": nrm((BATCH, SEQ, D_MODEL), 1.0),
        "x_sample": nrm((DEC_BATCH, DEC_SEQ, D_MODEL), 1.0),
        "state_ssm": nrm((DEPTH, DEC_BATCH, SSD_HEADS, SSD_HEAD_DIM, D_STATE), 0.1),
        "state_conv": nrm((DEPTH, DEC_BATCH, CONV_W - 1, CONV_DIM), 1.0),
        "cache_mem_k": nrm((DEPTH, DEC_BATCH, N_MEM, X_HEADS, X_HD), 1.0),
        "cache_mem_v": nrm((DEPTH, DEC_BATCH, N_MEM, X_HEADS, X_HD), 1.0),
        "mem_prompt": nrm((BATCH, N_MEM, D_MODEL), 1.0),
        "norm_mix": gain(D_MODEL),
        "w_in": nrm((DEPTH, D_MODEL, D_IN), D_MODEL ** -0.5),
        "gmlp_ln_g": gain(D_GMLP),
        "gmlp_ln_b": nrm((DEPTH, D_GMLP), 0.02),
        "gmlp_ws": nrm((DEPTH, GMLP_HEADS, CHUNK, CHUNK), CHUNK ** -0.5),
        "gmlp_bs": 1.0 + nrm((DEPTH, GMLP_HEADS, CHUNK), 0.02),
        "conv_w": nrm((DEPTH, CONV_W, CONV_DIM), CONV_W ** -0.5),
        "conv_b": nrm((DEPTH, CONV_DIM), 0.02),
        "dt_bias": dt_bias,
        "a_log": a_log,
        "d_skip": 1.0 + nrm((DEPTH, SSD_HEADS), 0.02),
        "ssd_norm": gain(D_SSD),
        "w_out": nrm((DEPTH, D_MIX, D_MODEL), D_MIX ** -0.5),
        "norm_xattn": gain(D_MODEL),
        "norm_mem": gain(D_MODEL),
        "w_q": nrm((DEPTH, D_MODEL, D_MODEL), D_MODEL ** -0.5),
        "w_k": nrm((DEPTH, D_MODEL, D_MODEL), D_MODEL ** -0.5),
        "w_v": nrm((DEPTH, D_MODEL, D_MODEL), D_MODEL ** -0.5),
        "w_xo": nrm((DEPTH, D_MODEL, D_MODEL), D_MODEL ** -0.5),
        "norm_ffn": gain(D_MODEL),
        "w_gate": nrm((DEPTH, D_MODEL, D_FF), D_MODEL ** -0.5),
        "w_up": nrm((DEPTH, D_MODEL, D_FF), D_MODEL ** -0.5),
        "w_down": nrm((DEPTH, D_FF, D_MODEL), D_FF ** -0.5),
        "norm_final": 1.0 + nrm((D_MODEL,), 0.02),
    }


def reference(x_prompt, x_sample, state_ssm, state_conv, cache_mem_k, cache_mem_v, mem_prompt,
              norm_mix, w_in, gmlp_ln_g, gmlp_ln_b, gmlp_ws, gmlp_bs, conv_w, conv_b, dt_bias,
              a_log, d_skip, ssd_norm, w_out, norm_xattn, norm_mem, w_q, w_k, w_v, w_xo,
              norm_ffn, w_gate, w_up, w_down, norm_final):
    hp, hs = x_prompt, x_sample
    ssm_p, conv_p, gv_p, mk_p, mv_p = [], [], [], [], []
    ssm_s, conv_s, gv_s = [], [], []
    for l in range(DEPTH):
        mix_w = (w_in[l], gmlp_ln_g[l], gmlp_ln_b[l], gmlp_ws[l], gmlp_bs[l], conv_w[l], conv_b[l],
                 dt_bias[l], a_log[l], d_skip[l], ssd_norm[l], w_out[l])
        zero_prefix = jnp.zeros((hp.shape[0], CONV_W - 1, CONV_DIM), hp.dtype)
        out, cst, hst, v = mixer_block(rmsnorm(hp, norm_mix[l]), zero_prefix, None, *mix_w)
        hp = hp + out
        mk, mv = mem_kv(mem_prompt, norm_mem[l], w_k[l], w_v[l])
        hp = hp + cross_attn(rmsnorm(hp, norm_xattn[l]), mk, mv, w_q[l], w_xo[l])
        hp = hp + swiglu(rmsnorm(hp, norm_ffn[l]), w_gate[l], w_up[l], w_down[l])
        ssm_p.append(hst); conv_p.append(cst); gv_p.append(v[:, -CHUNK:]); mk_p.append(mk); mv_p.append(mv)
        out, cst, hst, v = mixer_block(rmsnorm(hs, norm_mix[l]), state_conv[l], state_ssm[l], *mix_w)
        hs = hs + out
        hs = hs + cross_attn(rmsnorm(hs, norm_xattn[l]), cache_mem_k[l], cache_mem_v[l], w_q[l], w_xo[l])
        hs = hs + swiglu(rmsnorm(hs, norm_ffn[l]), w_gate[l], w_up[l], w_down[l])
        ssm_s.append(hst); conv_s.append(cst); gv_s.append(v)
    y_prompt = rmsnorm(hp, norm_final)
    y_sample = rmsnorm(hs, norm_final)
    return (y_prompt, y_sample, jnp.stack(ssm_p), jnp.stack(conv_p), jnp.stack(gv_p),
            jnp.stack(mk_p), jnp.stack(mv_p), jnp.stack(ssm_s), jnp.stack(conv_s), jnp.stack(gv_s))
```

```python
import functools

import numpy as np
import jax
import jax.numpy as jnp
from jax import lax
from jax.experimental import pallas as pl
from jax.experimental.pallas import tpu as pltpu

f32 = jnp.float32
bf16 = jnp.bfloat16

D_MODEL = 1024
BATCH = 8
SEQ = 2048
DEC_BATCH = 128
D_GMLP = 512
GMLP_HEADS = 4
GMLP_HD = 128
CHUNK = 128
D_SSD = 512
SSD_HEAD_DIM = 64
SSD_HEADS = 8
SSD_GROUPS = 2
D_STATE = 128
CONV_W = 4
CONV_DIM = 1024
N_MEM = 256
X_HEADS = 4
X_HD = 256
D_FF = 2816
EPS = 1e-6

N_TOK = BATCH * SEQ
N_CHUNKS = SEQ // CHUNK
TM = 512
LANES = 128
SUBLANES = 8
VMEM_LIMIT = 56 * 1024 * 1024
SB = 8


def _cparams(sem):
    return pltpu.CompilerParams(dimension_semantics=sem, vmem_limit_bytes=VMEM_LIMIT)


def _const_spec(shape):
    nd = len(shape)
    return pl.BlockSpec(shape, lambda *_: (0,) * nd, pipeline_mode=pl.Buffered(1))


def _whole_spec(shape):
    nd = len(shape)
    return pl.BlockSpec(shape, lambda *_: (0,) * nd)


def _dot(a, b):
    return jnp.dot(a, b, preferred_element_type=f32)


def _dot_nt(a, b):
    return lax.dot_general(a, b, (((1,), (1,)), ((), ())), preferred_element_type=f32)


def _dot_tn(a, b):
    return lax.dot_general(a, b, (((0,), (0,)), ((), ())), preferred_element_type=f32)


def _rmsnorm(x, g):
    return x * lax.rsqrt(jnp.mean(x * x, axis=-1, keepdims=True) + EPS) * g


def _layernorm(x, g, b):
    mu = jnp.mean(x, axis=-1, keepdims=True)
    xc = x - mu
    return xc * lax.rsqrt(jnp.mean(xc * xc, axis=-1, keepdims=True) + EPS) * g + b


def _silu(x):
    return x * jax.nn.sigmoid(x)


def _softplus(x):
    return jnp.maximum(x, 0.0) + jnp.log1p(jnp.exp(-jnp.abs(x)))


def _split_bf16(x, terms):
    parts = []
    r = x
    for _ in range(terms):
        p = r.astype(bf16)
        parts.append(p)
        r = r - p.astype(f32)
    return parts


def _select_dot(x, sel, terms):
    m = x.shape[0]
    stacked = jnp.concatenate(_split_bf16(x, terms), axis=0)
    r = _dot(stacked, sel)
    out = r[0:m]
    for i in range(1, terms):
        out = out + r[i * m:(i + 1) * m]
    return out


def _gated_group_norm(y, z, sn):
    yz = y * _silu(z)
    half = D_SSD // SSD_GROUPS
    outs = []
    for g in range(SSD_GROUPS):
        blk = yz[:, g * half:(g + 1) * half]
        outs.append(blk * lax.rsqrt(jnp.mean(blk * blk, axis=-1, keepdims=True) + EPS))
    return jnp.concatenate(outs, axis=1) * sn


def _in_proj(x, g, wu, wv, wz, wx, wdt, lng, lnb):
    xn = _rmsnorm(x, g).astype(bf16)
    gu = jax.nn.gelu(_dot(xn, wu))
    v = _layernorm(jax.nn.gelu(_dot(xn, wv)), lng, lnb)
    z = _dot(xn, wz)
    xbc = _dot(xn, wx)
    dtr = _dot(xn, wdt)
    return gu, v, z, xbc, dtr


def _p_inproj_body(x_ref, g_ref, wu_ref, wv_ref, wz_ref, wx_ref, wdt_ref, lng_ref, lnb_ref,
                   gu_ref, v_ref, gv_ref, z_ref, xbc_ref, dt_ref):
    gu, v, z, xbc, dtr = _in_proj(x_ref[...], g_ref[...], wu_ref[...], wv_ref[...], wz_ref[...],
                                  wx_ref[...], wdt_ref[...], lng_ref[...], lnb_ref[...])
    gu_ref[...] = gu
    v_ref[...] = v.astype(bf16)
    gv_ref[0] = v[TM - CHUNK:, :]
    z_ref[...] = z
    xbc_ref[...] = xbc
    dt_ref[...] = dtr


def _p_inproj(x2, g, wu, wv, wz, wx, wdt, lng, lnb):
    tiles = SEQ // TM
    row = lambda b, j: (b * tiles + j, 0)
    return pl.pallas_call(
        _p_inproj_body,
        grid=(BATCH, tiles),
        in_specs=[
            pl.BlockSpec((TM, D_MODEL), row),
            _const_spec((1, D_MODEL)),
            _const_spec((D_MODEL, D_GMLP)),
            _const_spec((D_MODEL, D_GMLP)),
            _const_spec((D_MODEL, D_SSD)),
            _const_spec((D_MODEL, CONV_DIM)),
            _const_spec((D_MODEL, LANES)),
            _const_spec((1, D_GMLP)),
            _const_spec((1, D_GMLP)),
        ],
        out_specs=[
            pl.BlockSpec((TM, D_GMLP), row),
            pl.BlockSpec((TM, D_GMLP), row),
            pl.BlockSpec((1, CHUNK, D_GMLP), lambda b, j: (b, 0, 0)),
            pl.BlockSpec((TM, D_SSD), row),
            pl.BlockSpec((TM, CONV_DIM), row),
            pl.BlockSpec((TM, LANES), row),
        ],
        out_shape=[
            jax.ShapeDtypeStruct((N_TOK, D_GMLP), f32),
            jax.ShapeDtypeStruct((N_TOK, D_GMLP), bf16),
            jax.ShapeDtypeStruct((BATCH, CHUNK, D_GMLP), f32),
            jax.ShapeDtypeStruct((N_TOK, D_SSD), f32),
            jax.ShapeDtypeStruct((N_TOK, CONV_DIM), f32),
            jax.ShapeDtypeStruct((N_TOK, LANES), f32),
        ],
        compiler_params=_cparams(("arbitrary", "arbitrary")),
        name="p_inproj",
    )(x2, g, wu, wv, wz, wx, wdt, lng, lnb)


def _p_mix_body(gu_ref, v_ref, z_ref, xbc_ref, dtr_ref,
                ws_ref, bsx_ref, cw_ref, cb_ref, dtb_ref, a_ref, dsk_ref, sn_ref,
                tri_ref, e64_ref, e128_ref,
                mix_ref, ssm_ref, conv_ref,
                xp_scr, h_scr):
    c = pl.program_id(1)

    @pl.when(c == 0)
    def _():
        xp_scr[0:SUBLANES, :] = jnp.zeros((SUBLANES, CONV_DIM), f32)
        h_scr[...] = jnp.zeros_like(h_scr)

    xbc = xbc_ref[...]
    xp_scr[SUBLANES:SUBLANES + CHUNK, :] = xbc
    acc = cb_ref[...]
    for k in range(CONV_W - 1):
        acc = acc + xp_scr[pl.ds(SUBLANES - (CONV_W - 1) + k, CHUNK), :] * cw_ref[k:k + 1, :]
    acc = acc + xbc * cw_ref[CONV_W - 1:CONV_W, :]
    xc = _silu(acc)
    tail = xbc[CHUNK - SUBLANES:, :]
    xp_scr[0:SUBLANES, :] = tail
    conv_ref[0] = tail

    xs = xc[:, :D_SSD]
    b_all = xc[:, D_SSD:D_SSD + SSD_GROUPS * D_STATE].astype(bf16)
    c_all = xc[:, D_SSD + SSD_GROUPS * D_STATE:].astype(bf16)

    dt = _softplus(dtr_ref[...] + dtb_ref[...])
    a = dt * a_ref[...]
    acum = _dot(tri_ref[...], jnp.concatenate(_split_bf16(a, 3), axis=1))
    acum = acum[:, 0:LANES] + acum[:, LANES:2 * LANES] + acum[:, 2 * LANES:3 * LANES]
    acum_t = acum.T
    a_last = acum[CHUNK - 1:CHUNK, :]
    expa = jnp.exp(acum)
    wgt = jnp.exp(a_last - acum) * dt

    per_head = _select_dot(jnp.concatenate([dt, expa, wgt], axis=0), e64_ref[...], 2)
    dt_e = per_head[0:CHUNK]
    expa_e = per_head[CHUNK:2 * CHUNK]
    wgt_e = per_head[2 * CHUNK:3 * CHUNK]
    col_b = _select_dot(acum, e128_ref[...], 3)

    t_idx = lax.broadcasted_iota(jnp.int32, (CHUNK, CHUNK), 0)
    s_idx = lax.broadcasted_iota(jnp.int32, (CHUNK, CHUNK), 1)
    causal = t_idx >= s_idx
    lane = lax.broadcasted_iota(jnp.int32, (CHUNK, LANES), 1)
    low_half = lane < SSD_HEAD_DIM

    cb = []
    for g in range(SSD_GROUPS):
        sl = slice(g * D_STATE, (g + 1) * D_STATE)
        cb.append(_dot_nt(c_all[:, sl], b_all[:, sl]))

    xdt = (xs * dt_e).astype(bf16)
    zero = jnp.zeros((CHUNK, LANES), bf16)
    y_diag = []
    for q in range(SSD_HEADS // 2):
        g = (2 * q) // (SSD_HEADS // SSD_GROUPS)
        w_pair = []
        for h in (2 * q, 2 * q + 1):
            seg = col_b[:, h * CHUNK:(h + 1) * CHUNK] - acum_t[h:h + 1, :]
            decay = jnp.exp(jnp.where(causal, seg, -jnp.inf))
            w_pair.append((cb[g] * decay).astype(bf16))
        xp = xdt[:, q * LANES:(q + 1) * LANES]
        rhs = jnp.concatenate([jnp.where(low_half, xp, zero), jnp.where(low_half, zero, xp)], axis=0)
        y_diag.append(_dot(jnp.concatenate(w_pair, axis=1), rhs))
    y_diag = jnp.concatenate(y_diag, axis=1)

    gw = D_SSD // SSD_GROUPS
    h_prev = h_scr[...].astype(bf16)
    y_off = jnp.concatenate(
        [_dot_nt(c_all[:, g * D_STATE:(g + 1) * D_STATE], h_prev[g * gw:(g + 1) * gw, :])
         for g in range(SSD_GROUPS)], axis=1) * expa_e

    xw = (xs * wgt_e).astype(bf16)
    states = [_dot_tn(xw[:, g * gw:(g + 1) * gw], b_all[:, g * D_STATE:(g + 1) * D_STATE])
              for g in range(SSD_GROUPS)]
    chunk_decay = jnp.exp(acum_t[:, CHUNK - 1:CHUNK])
    hpg = SSD_HEADS // SSD_GROUPS
    for h in range(SSD_HEADS):
        g, j = h // hpg, h % hpg
        rows = slice(h * SSD_HEAD_DIM, (h + 1) * SSD_HEAD_DIM)
        h_scr[rows, :] = (h_scr[rows, :] * chunk_decay[h:h + 1, :]
                          + states[g][j * SSD_HEAD_DIM:(j + 1) * SSD_HEAD_DIM, :])

    y = y_diag + y_off
    y = y + dsk_ref[...] * xs
    s_out = _gated_group_norm(y, z_ref[...], sn_ref[...])

    vb = v_ref[...]
    mixed = []
    for hh in range(GMLP_HEADS):
        w = jnp.where(causal, ws_ref[hh], 0.0).astype(bf16)
        mixed.append(_dot(w, vb[:, hh * GMLP_HD:(hh + 1) * GMLP_HD]))
    mixed = jnp.concatenate(mixed, axis=1) + bsx_ref[...]
    g_out = gu_ref[...] * mixed

    mix_ref[...] = jnp.concatenate([g_out, s_out], axis=1).astype(bf16)

    @pl.when(c == N_CHUNKS - 1)
    def _():
        ssm_ref[0] = h_scr[...]


def _p_mix(gu, v, z, xbc, dtr, ws, bsx, cw, cb, dtb, a_neg, dsk, sn, tri, e64, e128):
    row = lambda b, c: (b * N_CHUNKS + c, 0)
    per_b = lambda b, c: (b, 0, 0)
    return pl.pallas_call(
        _p_mix_body,
        grid=(BATCH, N_CHUNKS),
        in_specs=[
            pl.BlockSpec((CHUNK, D_GMLP), row),
            pl.BlockSpec((CHUNK, D_GMLP), row),
            pl.BlockSpec((CHUNK, D_SSD), row),
            pl.BlockSpec((CHUNK, CONV_DIM), row),
            pl.BlockSpec((CHUNK, LANES), row),
            _const_spec((GMLP_HEADS, CHUNK, CHUNK)),
            _const_spec((CHUNK, D_GMLP)),
            _const_spec((CONV_W, CONV_DIM)),
            _const_spec((1, CONV_DIM)),
            _const_spec((1, LANES)),
            _const_spec((1, LANES)),
            _const_spec((1, D_SSD)),
            _const_spec((1, D_SSD)),
            _const_spec((CHUNK, CHUNK)),
            _const_spec((LANES, D_SSD)),
            _const_spec((LANES, SSD_HEADS * CHUNK)),
        ],
        out_specs=[
            pl.BlockSpec((CHUNK, D_MODEL), row),
            pl.BlockSpec((1, D_SSD, D_STATE), per_b),
            pl.BlockSpec((1, SUBLANES, CONV_DIM), per_b),
        ],
        out_shape=[
            jax.ShapeDtypeStruct((N_TOK, D_MODEL), bf16),
            jax.ShapeDtypeStruct((BATCH, D_SSD, D_STATE), f32),
            jax.ShapeDtypeStruct((BATCH, SUBLANES, CONV_DIM), f32),
        ],
        scratch_shapes=[
            pltpu.VMEM((SUBLANES + CHUNK, CONV_DIM), f32),
            pltpu.VMEM((D_SSD, D_STATE), f32),
        ],
        compiler_params=_cparams(("arbitrary", "arbitrary")),
        name="p_mix",
    )(gu, v, z, xbc, dtr, ws, bsx, cw, cb, dtb, a_neg, dsk, sn, tri, e64, e128)


def _memkv_body(m_ref, g_ref, wk_ref, wv_ref, k_ref, v_ref):
    mn = _rmsnorm(m_ref[...], g_ref[...]).astype(bf16)
    k_ref[...] = _dot(mn, wk_ref[...])
    v_ref[...] = _dot(mn, wv_ref[...])


def _memkv(mem2, g, wk, wv):
    n = mem2.shape[0]
    row = lambda i: (i, 0)
    return pl.pallas_call(
        _memkv_body,
        grid=(n // TM,),
        in_specs=[
            pl.BlockSpec((TM, D_MODEL), row),
            _const_spec((1, D_MODEL)),
            _const_spec((D_MODEL, D_MODEL)),
            _const_spec((D_MODEL, D_MODEL)),
        ],
        out_specs=[pl.BlockSpec((TM, D_MODEL), row), pl.BlockSpec((TM, D_MODEL), row)],
        out_shape=[jax.ShapeDtypeStruct((n, D_MODEL), f32), jax.ShapeDtypeStruct((n, D_MODEL), f32)],
        compiler_params=_cparams(("arbitrary",)),
        name="memkv",
    )(mem2, g, wk, wv)


def _p_attn_body(x_ref, mix_ref, wo_ref, gx_ref, wq_ref, k_ref, v_ref, wxo_ref, out_ref):
    hp = x_ref[...] + _dot(mix_ref[...], wo_ref[...])
    hn = _rmsnorm(hp, gx_ref[...]).astype(bf16)
    q = _dot(hn, wq_ref[...]).astype(bf16)
    kb = k_ref[0].astype(bf16)
    vb = v_ref[0].astype(bf16)
    outs = []
    for h in range(X_HEADS):
        sl = slice(h * X_HD, (h + 1) * X_HD)
        s = _dot_nt(q[:, sl], kb[:, sl]) * (X_HD ** -0.5)
        e = jnp.exp(s - jnp.max(s, axis=-1, keepdims=True))
        p = e / jnp.sum(e, axis=-1, keepdims=True)
        outs.append(_dot(p.astype(bf16), vb[:, sl]))
    o = jnp.concatenate(outs, axis=1).astype(bf16)
    out_ref[...] = hp + _dot(o, wxo_ref[...])


def _p_attn(x2, mix, wo, gx, wq, mk, mv, wxo):
    tiles = SEQ // TM
    row = lambda b, j: (b * tiles + j, 0)
    per_b = lambda b, j: (b, 0, 0)
    return pl.pallas_call(
        _p_attn_body,
        grid=(BATCH, tiles),
        in_specs=[
            pl.BlockSpec((TM, D_MODEL), row),
            pl.BlockSpec((TM, D_MODEL), row),
            _const_spec((D_MODEL, D_MODEL)),
            _const_spec((1, D_MODEL)),
            _const_spec((D_MODEL, D_MODEL)),
            pl.BlockSpec((1, N_MEM, D_MODEL), per_b),
            pl.BlockSpec((1, N_MEM, D_MODEL), per_b),
            _const_spec((D_MODEL, D_MODEL)),
        ],
        out_specs=pl.BlockSpec((TM, D_MODEL), row),
        out_shape=jax.ShapeDtypeStruct((N_TOK, D_MODEL), f32),
        compiler_params=_cparams(("arbitrary", "arbitrary")),
        name="p_attn",
    )(x2, mix, wo, gx, wq, mk, mv, wxo)


def _ffn(hp, gf, wg, wu, wd, gfin):
    hn = _rmsnorm(hp, gf).astype(bf16)
    hid = (_silu(_dot(hn, wg)) * _dot(hn, wu)).astype(bf16)
    return _rmsnorm(hp + _dot(hid, wd), gfin)


def _ffn_body(h_ref, gf_ref, wg_ref, wu_ref, wd_ref, gfin_ref, y_ref):
    y_ref[...] = _ffn(h_ref[...], gf_ref[...], wg_ref[...], wu_ref[...], wd_ref[...], gfin_ref[...])


def _p_ffn(hp, gf, wg, wu, wd, gfin):
    row = lambda i: (i, 0)
    return pl.pallas_call(
        _ffn_body,
        grid=(N_TOK // TM,),
        in_specs=[
            pl.BlockSpec((TM, D_MODEL), row),
            _const_spec((1, D_MODEL)),
            _const_spec((D_MODEL, D_FF)),
            _const_spec((D_MODEL, D_FF)),
            _const_spec((D_FF, D_MODEL)),
            _const_spec((1, D_MODEL)),
        ],
        out_specs=pl.BlockSpec((TM, D_MODEL), row),
        out_shape=jax.ShapeDtypeStruct((N_TOK, D_MODEL), f32),
        compiler_params=_cparams(("arbitrary",)),
        name="p_ffn",
    )(hp, gf, wg, wu, wd, gfin)


def _s_inproj_body(x_ref, g_ref, wu_ref, wv_ref, wz_ref, wx_ref, wdt_ref, lng_ref, lnb_ref,
                   sc_ref, cw_ref, cb_ref, dtb_ref, a_ref, w00_ref, b0_ref, e64_ref,
                   gout_ref, v_ref, z_ref, conv_ref, xs_ref, xdt_ref, bc_ref, dec_ref):
    gu, v, z, xbc, dtr = _in_proj(x_ref[...], g_ref[...], wu_ref[...], wv_ref[...], wz_ref[...],
                                  wx_ref[...], wdt_ref[...], lng_ref[...], lnb_ref[...])
    v_ref[...] = v
    z_ref[...] = z
    gout_ref[...] = gu * (v * w00_ref[...] + b0_ref[...])

    acc = cb_ref[...]
    for k in range(CONV_W - 1):
        acc = acc + sc_ref[:, k * CONV_DIM:(k + 1) * CONV_DIM] * cw_ref[k:k + 1, :]
    acc = acc + xbc * cw_ref[CONV_W - 1:CONV_W, :]
    xc = _silu(acc)
    conv_ref[:, 0:(CONV_W - 2) * CONV_DIM] = sc_ref[:, CONV_DIM:(CONV_W - 1) * CONV_DIM]
    conv_ref[:, (CONV_W - 2) * CONV_DIM:] = xbc

    xs = xc[:, :D_SSD]
    dt = _softplus(dtr + dtb_ref[...])
    dec_ref[...] = jnp.exp(dt * a_ref[...])
    xs_ref[...] = xs
    xdt_ref[...] = xs * _select_dot(dt, e64_ref[...], 2)
    bc_ref[...] = xc[:, D_SSD:]


def _s_inproj(x, g, wu, wv, wz, wx, wdt, lng, lnb, sc, cw, cb, dtb, a_neg, w00, b0, e64):
    nb = DEC_BATCH
    args = (x, g, wu, wv, wz, wx, wdt, lng, lnb, sc, cw, cb, dtb, a_neg, w00, b0, e64)
    return pl.pallas_call(
        _s_inproj_body,
        grid=(1,),
        in_specs=[_const_spec(a.shape) for a in args],
        out_specs=[
            _whole_spec((nb, D_GMLP)), _whole_spec((nb, D_GMLP)), _whole_spec((nb, D_SSD)),
            _whole_spec((nb, (CONV_W - 1) * CONV_DIM)), _whole_spec((nb, D_SSD)),
            _whole_spec((nb, D_SSD)), _whole_spec((nb, 2 * SSD_GROUPS * D_STATE)),
            _whole_spec((nb, LANES)),
        ],
        out_shape=[
            jax.ShapeDtypeStruct((nb, D_GMLP), f32), jax.ShapeDtypeStruct((nb, D_GMLP), f32),
            jax.ShapeDtypeStruct((nb, D_SSD), f32),
            jax.ShapeDtypeStruct((nb, (CONV_W - 1) * CONV_DIM), f32),
            jax.ShapeDtypeStruct((nb, D_SSD), f32), jax.ShapeDtypeStruct((nb, D_SSD), f32),
            jax.ShapeDtypeStruct((nb, 2 * SSD_GROUPS * D_STATE), f32),
            jax.ShapeDtypeStruct((nb, LANES), f32),
        ],
        compiler_params=_cparams(("arbitrary",)),
        name="s_inproj",
    )(*args)


def _s_state_body(dec_ref, h0_ref, xdt_ref, bc_ref, h_ref, y_ref):
    step = pl.program_id(0)
    gw = D_SSD // SSD_GROUPS
    hpg = SSD_HEADS // SSD_GROUPS
    nbc = SSD_GROUPS * D_STATE
    xdt = xdt_ref[...]
    bc = bc_ref[...]
    row = lax.broadcasted_iota(jnp.int32, (SB, D_STATE), 0)
    y = [jnp.zeros((SB, gw), f32) for _ in range(SSD_GROUPS)]
    for i in range(SB):
        only_i = row == i
        for g in range(SSD_GROUPS):
            b_i = jnp.where(only_i, bc[:, g * D_STATE:(g + 1) * D_STATE], 0.0)
            c_i = jnp.where(only_i, bc[:, nbc + g * D_STATE:nbc + (g + 1) * D_STATE], 0.0)
            upd = _dot_tn(xdt[:, g * gw:(g + 1) * gw], b_i)
            blocks = []
            for j in range(hpg):
                h = g * hpg + j
                rows = slice(h * SSD_HEAD_DIM, (h + 1) * SSD_HEAD_DIM)
                d = dec_ref[(step * SB + i) * SSD_HEADS + h]
                blocks.append(h0_ref[i, rows, :] * d + upd[j * SSD_HEAD_DIM:(j + 1) * SSD_HEAD_DIM, :])
            h_new = jnp.concatenate(blocks, axis=0)
            h_ref[i, g * gw:(g + 1) * gw, :] = h_new
            y[g] = y[g] + _dot_nt(c_i.astype(bf16), h_new.astype(bf16))
    y_ref[...] = jnp.concatenate(y, axis=1)


def _s_state(dec, h0, xdt, bc):
    blk = lambda i: (i, 0, 0)
    row = lambda i: (i, 0)
    return pl.pallas_call(
        _s_state_body,
        grid=(DEC_BATCH // SB,),
        in_specs=[
            pl.BlockSpec(memory_space=pltpu.SMEM),
            pl.BlockSpec((SB, D_SSD, D_STATE), blk),
            pl.BlockSpec((SB, D_SSD), row),
            pl.BlockSpec((SB, 2 * SSD_GROUPS * D_STATE), row),
        ],
        out_specs=[pl.BlockSpec((SB, D_SSD, D_STATE), blk), pl.BlockSpec((SB, D_SSD), row)],
        out_shape=[jax.ShapeDtypeStruct((DEC_BATCH, D_SSD, D_STATE), f32),
                   jax.ShapeDtypeStruct((DEC_BATCH, D_SSD), f32)],
        compiler_params=_cparams(("arbitrary",)),
        name="s_state",
    )(dec, h0, xdt, bc)


def _s_mixout_body(x_ref, gout_ref, y_ref, xs_ref, z_ref, dsk_ref, sn_ref, wo_ref, gx_ref, wq_ref,
                   hs_ref, q_ref):
    y = y_ref[...] + dsk_ref[...] * xs_ref[...]
    s_out = _gated_group_norm(y, z_ref[...], sn_ref[...])
    mix = jnp.concatenate([gout_ref[...], s_out], axis=1).astype(bf16)
    hs = x_ref[...] + _dot(mix, wo_ref[...])
    hs_ref[...] = hs
    q_ref[...] = _dot(_rmsnorm(hs, gx_ref[...]).astype(bf16), wq_ref[...])


def _s_mixout(x, gout, y, xs, z, dsk, sn, wo, gx, wq):
    args = (x, gout, y, xs, z, dsk, sn, wo, gx, wq)
    return pl.pallas_call(
        _s_mixout_body,
        grid=(1,),
        in_specs=[_const_spec(a.shape) for a in args],
        out_specs=[_whole_spec((DEC_BATCH, D_MODEL)), _whole_spec((DEC_BATCH, D_MODEL))],
        out_shape=[jax.ShapeDtypeStruct((DEC_BATCH, D_MODEL), f32),
                   jax.ShapeDtypeStruct((DEC_BATCH, D_MODEL), f32)],
        compiler_params=_cparams(("arbitrary",)),
        name="s_mixout",
    )(*args)


def _s_attn_body(q_ref, k_ref, v_ref, o_ref):
    def one(i, carry):
        for h in range(X_HEADS):
            sl = slice(h * X_HD, (h + 1) * X_HD)
            qh = q_ref[pl.ds(i, 1), sl]
            s = jnp.sum(k_ref[i, :, sl] * qh, axis=-1, keepdims=True) * (X_HD ** -0.5)
            e = jnp.exp(s - jnp.max(s, axis=0, keepdims=True))
            p = e / jnp.sum(e, axis=0, keepdims=True)
            o_ref[pl.ds(i, 1), sl] = jnp.sum(p * v_ref[i, :, sl], axis=0, keepdims=True)
        return carry

    lax.fori_loop(0, SB, one, 0)


def _s_attn(q, ck, cv):
    blk = lambda i: (i, 0, 0)
    row = lambda i: (i, 0)
    return pl.pallas_call(
        _s_attn_body,
        grid=(DEC_BATCH // SB,),
        in_specs=[
            pl.BlockSpec((SB, D_MODEL), row),
            pl.BlockSpec((SB, N_MEM, D_MODEL), blk),
            pl.BlockSpec((SB, N_MEM, D_MODEL), blk),
        ],
        out_specs=pl.BlockSpec((SB, D_MODEL), row),
        out_shape=jax.ShapeDtypeStruct((DEC_BATCH, D_MODEL), f32),
        compiler_params=_cparams(("arbitrary",)),
        name="s_attn",
    )(q, ck, cv)


def _s_ffn_body(hs_ref, o_ref, wxo_ref, gf_ref, wg_ref, wu_ref, wd_ref, gfin_ref, y_ref):
    hs = hs_ref[...] + _dot(o_ref[...].astype(bf16), wxo_ref[...])
    y_ref[...] = _ffn(hs, gf_ref[...], wg_ref[...], wu_ref[...], wd_ref[...], gfin_ref[...])


def _s_ffn(hs, o, wxo, gf, wg, wu, wd, gfin):
    args = (hs, o, wxo, gf, wg, wu, wd, gfin)
    return pl.pallas_call(
        _s_ffn_body,
        grid=(1,),
        in_specs=[_const_spec(a.shape) for a in args],
        out_specs=_whole_spec((DEC_BATCH, D_MODEL)),
        out_shape=jax.ShapeDtypeStruct((DEC_BATCH, D_MODEL), f32),
        compiler_params=_cparams(("arbitrary",)),
        name="s_ffn",
    )(*args)


def _tri_ones():
    return jnp.asarray(np.tril(np.ones((CHUNK, CHUNK), np.float32)), bf16)


def _head_select(width):
    e = np.zeros((LANES, SSD_HEADS * width), np.float32)
    for h in range(SSD_HEADS):
        e[h, h * width:(h + 1) * width] = 1.0
    return jnp.asarray(e, bf16)


def _pad_lanes(v):
    return jnp.pad(v.astype(f32), (0, LANES - v.shape[0]))[None, :]


def kernel(x_prompt, x_sample, state_ssm, state_conv, cache_mem_k, cache_mem_v, mem_prompt,
           norm_mix, w_in, gmlp_ln_g, gmlp_ln_b, gmlp_ws, gmlp_bs, conv_w, conv_b, dt_bias,
           a_log, d_skip, ssd_norm, w_out, norm_xattn, norm_mem, w_q, w_k, w_v, w_xo,
           norm_ffn, w_gate, w_up, w_down, norm_final):
    l = 0
    wi = w_in[l].astype(bf16)
    o0, o1, o2, o3 = D_GMLP, 2 * D_GMLP, 2 * D_GMLP + D_SSD, 2 * D_GMLP + D_SSD + CONV_DIM
    wu_in, wv_in, wz_in, wx_in = wi[:, :o0], wi[:, o0:o1], wi[:, o1:o2], wi[:, o2:o3]
    wdt_in = jnp.pad(wi[:, o3:], ((0, 0), (0, LANES - SSD_HEADS)))
    wo = w_out[l].astype(bf16)
    wq, wk, wv, wxo = (w[l].astype(bf16) for w in (w_q, w_k, w_v, w_xo))
    wg, wu, wd = (w[l].astype(bf16) for w in (w_gate, w_up, w_down))

    g_mix, g_x, g_mem, g_ffn = (g[l][None, :] for g in (norm_mix, norm_xattn, norm_mem, norm_ffn))
    g_fin = norm_final[None, :]
    lng, lnb = gmlp_ln_g[l][None, :], gmlp_ln_b[l][None, :]
    cw, cb = conv_w[l], conv_b[l][None, :]
    dtb = _pad_lanes(dt_bias[l])
    a_neg = _pad_lanes(-jnp.exp(a_log[l].astype(f32)))
    dsk = jnp.repeat(d_skip[l].astype(f32), SSD_HEAD_DIM)[None, :]
    sn = ssd_norm[l][None, :]
    ws = gmlp_ws[l]
    bsx = jnp.repeat(jnp.swapaxes(gmlp_bs[l], 0, 1), GMLP_HD, axis=1)
    w00 = jnp.repeat(ws[:, 0, 0], GMLP_HD)[None, :]
    b0 = bsx[0:1, :]
    tri, e64, e128 = _tri_ones(), _head_select(SSD_HEAD_DIM), _head_select(CHUNK)

    x2 = x_prompt.reshape(N_TOK, D_MODEL)
    gu, v, gv_p, z, xbc, dtr = _p_inproj(x2, g_mix, wu_in, wv_in, wz_in, wx_in, wdt_in, lng, lnb)
    mix, ssm_p, conv8 = _p_mix(gu, v, z, xbc, dtr, ws, bsx, cw, cb, dtb, a_neg, dsk, sn, tri, e64, e128)
    mk, mv = _memkv(mem_prompt.reshape(BATCH * N_MEM, D_MODEL), g_mem, wk, wv)
    mk3, mv3 = mk.reshape(BATCH, N_MEM, D_MODEL), mv.reshape(BATCH, N_MEM, D_MODEL)
    hp = _p_attn(x2, mix, wo, g_x, wq, mk3, mv3, wxo)
    y_prompt = _p_ffn(hp, g_ffn, wg, wu, wd, g_fin).reshape(BATCH, SEQ, D_MODEL)

    xs2 = x_sample.reshape(DEC_BATCH, D_MODEL)
    sc = state_conv[l].reshape(DEC_BATCH, (CONV_W - 1) * CONV_DIM)
    gout, v_s, z_s, conv_s, xs_s, xdt_s, bc_s, dec_s = _s_inproj(
        xs2, g_mix, wu_in, wv_in, wz_in, wx_in, wdt_in, lng, lnb, sc, cw, cb, dtb, a_neg, w00, b0, e64)
    h0 = state_ssm[l].reshape(DEC_BATCH, D_SSD, D_STATE)
    ssm_s, y_s = _s_state(dec_s[:, :SSD_HEADS].reshape(-1), h0, xdt_s, bc_s)
    hs, q_s = _s_mixout(xs2, gout, y_s, xs_s, z_s, dsk, sn, wo, g_x, wq)
    ck = cache_mem_k[l].reshape(DEC_BATCH, N_MEM, D_MODEL)
    cv = cache_mem_v[l].reshape(DEC_BATCH, N_MEM, D_MODEL)
    o_s = _s_attn(q_s, ck, cv)
    y_sample = _s_ffn(hs, o_s, wxo, g_ffn, wg, wu, wd, g_fin).reshape(DEC_BATCH, 1, D_MODEL)

    return (
        y_prompt,
        y_sample,
        ssm_p.reshape(1, BATCH, SSD_HEADS, SSD_HEAD_DIM, D_STATE),
        conv8[:, SUBLANES - (CONV_W - 1):, :][None],
        gv_p[None],
        mk.reshape(1, BATCH, N_MEM, X_HEADS, X_HD),
        mv.reshape(1, BATCH, N_MEM, X_HEADS, X_HD),
        ssm_s.reshape(1, DEC_BATCH, SSD_HEADS, SSD_HEAD_DIM, D_STATE),
        conv_s.reshape(1, DEC_BATCH, CONV_W - 1, CONV_DIM),
        v_s.reshape(1, DEC_BATCH, 1, D_GMLP),
    )
```

```python
import functools

import numpy as np
import jax
import jax.numpy as jnp
from jax import lax
from jax.experimental import pallas as pl
from jax.experimental.pallas import tpu as pltpu

f32 = jnp.float32
bf16 = jnp.bfloat16

D_MODEL = 1024
BATCH = 8
SEQ = 2048
DEC_BATCH = 128
D_GMLP = 512
GMLP_HEADS = 4
GMLP_HD = 128
CHUNK = 128
D_SSD = 512
SSD_HEAD_DIM = 64
SSD_HEADS = 8
SSD_GROUPS = 2
D_STATE = 128
CONV_W = 4
CONV_DIM = 1024
N_MEM = 256
X_HEADS = 4
X_HD = 256
D_FF = 2816
EPS = 1e-6

N_TOK = BATCH * SEQ
N_CHUNKS = SEQ // CHUNK
TM = 512
LANES = 128
SUBLANES = 8
VMEM_LIMIT = 56 * 1024 * 1024
SB = 8


def _cparams(sem):
    return pltpu.CompilerParams(dimension_semantics=sem, vmem_limit_bytes=VMEM_LIMIT)


def _const_spec(shape):
    nd = len(shape)
    return pl.BlockSpec(shape, lambda *_: (0,) * nd, pipeline_mode=pl.Buffered(1))


def _whole_spec(shape):
    nd = len(shape)
    return pl.BlockSpec(shape, lambda *_: (0,) * nd)


def _dot(a, b):
    return jnp.dot(a, b, preferred_element_type=f32)


def _dot_nt(a, b):
    return lax.dot_general(a, b, (((1,), (1,)), ((), ())), preferred_element_type=f32)


def _dot_tn(a, b):
    return lax.dot_general(a, b, (((0,), (0,)), ((), ())), preferred_element_type=f32)


def _rmsnorm(x, g):
    return x * lax.rsqrt(jnp.mean(x * x, axis=-1, keepdims=True) + EPS) * g


def _layernorm(x, g, b):
    mu = jnp.mean(x, axis=-1, keepdims=True)
    xc = x - mu
    return xc * lax.rsqrt(jnp.mean(xc * xc, axis=-1, keepdims=True) + EPS) * g + b


def _silu(x):
    return x * jax.nn.sigmoid(x)


def _softplus(x):
    return jnp.maximum(x, 0.0) + jnp.log1p(jnp.exp(-jnp.abs(x)))


def _split_bf16(x, terms):
    parts = []
    r = x
    for _ in range(terms):
        p = r.astype(bf16)
        parts.append(p)
        r = r - p.astype(f32)
    return parts


def _select_dot(x, sel, terms):
    m = x.shape[0]
    stacked = jnp.concatenate(_split_bf16(x, terms), axis=0)
    r = _dot(stacked, sel)
    out = r[0:m]
    for i in range(1, terms):
        out = out + r[i * m:(i + 1) * m]
    return out


def _gated_group_norm(y, z, sn):
    yz = y * _silu(z)
    half = D_SSD // SSD_GROUPS
    outs = []
    for g in range(SSD_GROUPS):
        blk = yz[:, g * half:(g + 1) * half]
        outs.append(blk * lax.rsqrt(jnp.mean(blk * blk, axis=-1, keepdims=True) + EPS))
    return jnp.concatenate(outs, axis=1) * sn


def _in_proj(x, g, wu, wv, wz, wx, wdt, lng, lnb):
    xn = _rmsnorm(x, g).astype(bf16)
    gu = jax.nn.gelu(_dot(xn, wu))
    v = _layernorm(jax.nn.gelu(_dot(xn, wv)), lng, lnb)
    z = _dot(xn, wz)
    xbc = _dot(xn, wx)
    dtr = _dot(xn, wdt)
    return gu, v, z, xbc, dtr


def _p_inproj_body(x_ref, g_ref, wu_ref, wv_ref, wz_ref, wx_ref, wdt_ref, lng_ref, lnb_ref,
                   gu_ref, v_ref, gv_ref, z_ref, xbc_ref, dt_ref):
    gu, v, z, xbc, dtr = _in_proj(x_ref[...], g_ref[...], wu_ref[...], wv_ref[...], wz_ref[...],
                                  wx_ref[...], wdt_ref[...], lng_ref[...], lnb_ref[...])
    gu_ref[...] = gu
    v_ref[...] = v.astype(bf16)
    gv_ref[0] = v[TM - CHUNK:, :]
    z_ref[...] = z
    xbc_ref[...] = xbc
    dt_ref[...] = dtr


def _p_inproj(x2, g, wu, wv, wz, wx, wdt, lng, lnb):
    tiles = SEQ // TM
    row = lambda b, j: (b * tiles + j, 0)
    return pl.pallas_call(
        _p_inproj_body,
        grid=(BATCH, tiles),
        in_specs=[
            pl.BlockSpec((TM, D_MODEL), row),
            _const_spec((1, D_MODEL)),
            _const_spec((D_MODEL, D_GMLP)),
            _const_spec((D_MODEL, D_GMLP)),
            _const_spec((D_MODEL, D_SSD)),
            _const_spec((D_MODEL, CONV_DIM)),
            _const_spec((D_MODEL, LANES)),
            _const_spec((1, D_GMLP)),
            _const_spec((1, D_GMLP)),
        ],
        out_specs=[
            pl.BlockSpec((TM, D_GMLP), row),
            pl.BlockSpec((TM, D_GMLP), row),
            pl.BlockSpec((1, CHUNK, D_GMLP), lambda b, j: (b, 0, 0)),
            pl.BlockSpec((TM, D_SSD), row),
            pl.BlockSpec((TM, CONV_DIM), row),
            pl.BlockSpec((TM, LANES), row),
        ],
        out_shape=[
            jax.ShapeDtypeStruct((N_TOK, D_GMLP), f32),
            jax.ShapeDtypeStruct((N_TOK, D_GMLP), bf16),
            jax.ShapeDtypeStruct((BATCH, CHUNK, D_GMLP), f32),
            jax.ShapeDtypeStruct((N_TOK, D_SSD), f32),
            jax.ShapeDtypeStruct((N_TOK, CONV_DIM), f32),
            jax.ShapeDtypeStruct((N_TOK, LANES), f32),
        ],
        compiler_params=_cparams(("arbitrary", "arbitrary")),
        name="p_inproj",
    )(x2, g, wu, wv, wz, wx, wdt, lng, lnb)


def _p_mix_body(gu_ref, v_ref, z_ref, xbc_ref, dtr_ref,
                ws_ref, bsx_ref, cw_ref, cb_ref, dtb_ref, a_ref, dsk_ref, sn_ref,
                tri_ref, e64_ref, e128_ref,
                mix_ref, ssm_ref, conv_ref,
                xp_scr, h_scr):
    c = pl.program_id(1)

    @pl.when(c == 0)
    def _():
        xp_scr[0:SUBLANES, :] = jnp.zeros((SUBLANES, CONV_DIM), f32)
        h_scr[...] = jnp.zeros_like(h_scr)

    xbc = xbc_ref[...]
    xp_scr[SUBLANES:SUBLANES + CHUNK, :] = xbc
    acc = cb_ref[...]
    for k in range(CONV_W - 1):
        acc = acc + xp_scr[pl.ds(SUBLANES - (CONV_W - 1) + k, CHUNK), :] * cw_ref[k:k + 1, :]
    acc = acc + xbc * cw_ref[CONV_W - 1:CONV_W, :]
    xc = _silu(acc)
    tail = xbc[CHUNK - SUBLANES:, :]
    xp_scr[0:SUBLANES, :] = tail
    conv_ref[0] = tail

    xs = xc[:, :D_SSD]
    b_all = xc[:, D_SSD:D_SSD + SSD_GROUPS * D_STATE].astype(bf16)
    c_all = xc[:, D_SSD + SSD_GROUPS * D_STATE:].astype(bf16)

    dt = _softplus(dtr_ref[...] + dtb_ref[...])
    a = dt * a_ref[...]
    acum = _dot(tri_ref[...], jnp.concatenate(_split_bf16(a, 3), axis=1))
    acum = acum[:, 0:LANES] + acum[:, LANES:2 * LANES] + acum[:, 2 * LANES:3 * LANES]
    acum_t = acum.T
    a_last = acum[CHUNK - 1:CHUNK, :]
    expa = jnp.exp(acum)
    wgt = jnp.exp(a_last - acum) * dt

    per_head = _select_dot(jnp.concatenate([dt, expa, wgt], axis=0), e64_ref[...], 2)
    dt_e = per_head[0:CHUNK]
    expa_e = per_head[CHUNK:2 * CHUNK]
    wgt_e = per_head[2 * CHUNK:3 * CHUNK]
    col_b = _select_dot(acum, e128_ref[...], 3)

    t_idx = lax.broadcasted_iota(jnp.int32, (CHUNK, CHUNK), 0)
    s_idx = lax.broadcasted_iota(jnp.int32, (CHUNK, CHUNK), 1)
    causal = t_idx >= s_idx
    lane = lax.broadcasted_iota(jnp.int32, (CHUNK, LANES), 1)
    low_half = lane < SSD_HEAD_DIM

    cb = []
    for g in range(SSD_GROUPS):
        sl = slice(g * D_STATE, (g + 1) * D_STATE)
        cb.append(_dot_nt(c_all[:, sl], b_all[:, sl]))

    xdt = (xs * dt_e).astype(bf16)
    zero = jnp.zeros((CHUNK, LANES), bf16)
    y_diag = []
    for q in range(SSD_HEADS // 2):
        g = (2 * q) // (SSD_HEADS // SSD_GROUPS)
        w_pair = []
        for h in (2 * q, 2 * q + 1):
            seg = col_b[:, h * CHUNK:(h + 1) * CHUNK] - acum_t[h:h + 1, :]
            decay = jnp.exp(jnp.where(causal, seg, -jnp.inf))
            w_pair.append((cb[g] * decay).astype(bf16))
        xp = xdt[:, q * LANES:(q + 1) * LANES]
        rhs = jnp.concatenate([jnp.where(low_half, xp, zero), jnp.where(low_half, zero, xp)], axis=0)
        y_diag.append(_dot(jnp.concatenate(w_pair, axis=1), rhs))
    y_diag = jnp.concatenate(y_diag, axis=1)

    gw = D_SSD // SSD_GROUPS
    h_prev = h_scr[...].astype(bf16)
    y_off = jnp.concatenate(
        [_dot_nt(c_all[:, g * D_STATE:(g + 1) * D_STATE], h_prev[g * gw:(g + 1) * gw, :])
         for g in range(SSD_GROUPS)], axis=1) * expa_e

    xw = (xs * wgt_e).astype(bf16)
    states = [_dot_tn(xw[:, g * gw:(g + 1) * gw], b_all[:, g * D_STATE:(g + 1) * D_STATE])
              for g in range(SSD_GROUPS)]
    chunk_decay = jnp.exp(acum_t[:, CHUNK - 1:CHUNK])
    hpg = SSD_HEADS // SSD_GROUPS
    for h in range(SSD_HEADS):
        g, j = h // hpg, h % hpg
        rows = slice(h * SSD_HEAD_DIM, (h + 1) * SSD_HEAD_DIM)
        h_scr[rows, :] = (h_scr[rows, :] * chunk_decay[h:h + 1, :]
                          + states[g][j * SSD_HEAD_DIM:(j + 1) * SSD_HEAD_DIM, :])

    y = y_diag + y_off
    y = y + dsk_ref[...] * xs
    s_out = _gated_group_norm(y, z_ref[...], sn_ref[...])

    vb = v_ref[...]
    mixed = []
    for hh in range(GMLP_HEADS):
        w = jnp.where(causal, ws_ref[hh], 0.0).astype(bf16)
        mixed.append(_dot(w, vb[:, hh * GMLP_HD:(hh + 1) * GMLP_HD]))
    mixed = jnp.concatenate(mixed, axis=1) + bsx_ref[...]
    g_out = gu_ref[...] * mixed

    mix_ref[...] = jnp.concatenate([g_out, s_out], axis=1).astype(bf16)

    @pl.when(c == N_CHUNKS - 1)
    def _():
        ssm_ref[0] = h_scr[...]


def _p_mix(gu, v, z, xbc, dtr, ws, bsx, cw, cb, dtb, a_neg, dsk, sn, tri, e64, e128):
    row = lambda b, c: (b * N_CHUNKS + c, 0)
    per_b = lambda b, c: (b, 0, 0)
    return pl.pallas_call(
        _p_mix_body,
        grid=(BATCH, N_CHUNKS),
        in_specs=[
            pl.BlockSpec((CHUNK, D_GMLP), row),
            pl.BlockSpec((CHUNK, D_GMLP), row),
            pl.BlockSpec((CHUNK, D_SSD), row),
            pl.BlockSpec((CHUNK, CONV_DIM), row),
            pl.BlockSpec((CHUNK, LANES), row),
            _const_spec((GMLP_HEADS, CHUNK, CHUNK)),
            _const_spec((CHUNK, D_GMLP)),
            _const_spec((CONV_W, CONV_DIM)),
            _const_spec((1, CONV_DIM)),
            _const_spec((1, LANES)),
            _const_spec((1, LANES)),
            _const_spec((1, D_SSD)),
            _const_spec((1, D_SSD)),
            _const_spec((CHUNK, CHUNK)),
            _const_spec((LANES, D_SSD)),
            _const_spec((LANES, SSD_HEADS * CHUNK)),
        ],
        out_specs=[
            pl.BlockSpec((CHUNK, D_MODEL), row),
            pl.BlockSpec((1, D_SSD, D_STATE), per_b),
            pl.BlockSpec((1, SUBLANES, CONV_DIM), per_b),
        ],
        out_shape=[
            jax.ShapeDtypeStruct((N_TOK, D_MODEL), bf16),
            jax.ShapeDtypeStruct((BATCH, D_SSD, D_STATE), f32),
            jax.ShapeDtypeStruct((BATCH, SUBLANES, CONV_DIM), f32),
        ],
        scratch_shapes=[
            pltpu.VMEM((SUBLANES + CHUNK, CONV_DIM), f32),
            pltpu.VMEM((D_SSD, D_STATE), f32),
        ],
        compiler_params=_cparams(("arbitrary", "arbitrary")),
        name="p_mix",
    )(gu, v, z, xbc, dtr, ws, bsx, cw, cb, dtb, a_neg, dsk, sn, tri, e64, e128)


def _memkv_body(m_ref, g_ref, wk_ref, wv_ref, k4_ref, v4_ref, kb_ref, vb_ref):
    mn = _rmsnorm(m_ref[...], g_ref[...]).astype(bf16)
    for w_ref, o4_ref, ob_ref in ((wk_ref, k4_ref, kb_ref), (wv_ref, v4_ref, vb_ref)):
        r = _dot(mn, w_ref[...])
        ob_ref[...] = r.astype(bf16)
        for h in range(X_HEADS):
            o4_ref[:, h, :] = r[:, h * X_HD:(h + 1) * X_HD]


def _memkv(mem2, g, wk, wv):
    n = mem2.shape[0]
    row = lambda i: (i, 0)
    row4 = lambda i: (i, 0, 0)
    spec4 = pl.BlockSpec((TM, X_HEADS, X_HD), row4)
    spec2 = pl.BlockSpec((TM, D_MODEL), row)
    shape4 = jax.ShapeDtypeStruct((n, X_HEADS, X_HD), f32)
    shape2 = jax.ShapeDtypeStruct((n, D_MODEL), bf16)
    return pl.pallas_call(
        _memkv_body,
        grid=(n // TM,),
        in_specs=[
            spec2,
            _const_spec((1, D_MODEL)),
            _const_spec((D_MODEL, D_MODEL)),
            _const_spec((D_MODEL, D_MODEL)),
        ],
        out_specs=[spec4, spec4, spec2, spec2],
        out_shape=[shape4, shape4, shape2, shape2],
        compiler_params=_cparams(("arbitrary",)),
        name="memkv",
    )(mem2, g, wk, wv)


def _p_attn_body(x_ref, mix_ref, wo_ref, gx_ref, wq_ref, k_ref, v_ref, wxo_ref, out_ref):
    hp = x_ref[...] + _dot(mix_ref[...], wo_ref[...])
    hn = _rmsnorm(hp, gx_ref[...]).astype(bf16)
    q = _dot(hn, wq_ref[...]).astype(bf16)
    kb = k_ref[0]
    vb = v_ref[0]
    outs = []
    for h in range(X_HEADS):
        sl = slice(h * X_HD, (h + 1) * X_HD)
        s = _dot_nt(q[:, sl], kb[:, sl]) * (X_HD ** -0.5)
        e = jnp.exp(s - jnp.max(s, axis=-1, keepdims=True))
        p = e / jnp.sum(e, axis=-1, keepdims=True)
        outs.append(_dot(p.astype(bf16), vb[:, sl]))
    o = jnp.concatenate(outs, axis=1).astype(bf16)
    out_ref[...] = hp + _dot(o, wxo_ref[...])


def _p_attn(x2, mix, wo, gx, wq, mk, mv, wxo):
    tiles = SEQ // TM
    row = lambda b, j: (b * tiles + j, 0)
    per_b = lambda b, j: (b, 0, 0)
    return pl.pallas_call(
        _p_attn_body,
        grid=(BATCH, tiles),
        in_specs=[
            pl.BlockSpec((TM, D_MODEL), row),
            pl.BlockSpec((TM, D_MODEL), row),
            _const_spec((D_MODEL, D_MODEL)),
            _const_spec((1, D_MODEL)),
            _const_spec((D_MODEL, D_MODEL)),
            pl.BlockSpec((1, N_MEM, D_MODEL), per_b),
            pl.BlockSpec((1, N_MEM, D_MODEL), per_b),
            _const_spec((D_MODEL, D_MODEL)),
        ],
        out_specs=pl.BlockSpec((TM, D_MODEL), row),
        out_shape=jax.ShapeDtypeStruct((N_TOK, D_MODEL), f32),
        compiler_params=_cparams(("arbitrary", "arbitrary")),
        name="p_attn",
    )(x2, mix, wo, gx, wq, mk, mv, wxo)


def _ffn(hp, gf, wg, wu, wd, gfin):
    hn = _rmsnorm(hp, gf).astype(bf16)
    hid = (_silu(_dot(hn, wg)) * _dot(hn, wu)).astype(bf16)
    return _rmsnorm(hp + _dot(hid, wd), gfin)


def _ffn_body(h_ref, gf_ref, wg_ref, wu_ref, wd_ref, gfin_ref, y_ref):
    y_ref[...] = _ffn(h_ref[...], gf_ref[...], wg_ref[...], wu_ref[...], wd_ref[...], gfin_ref[...])


def _p_ffn(hp, gf, wg, wu, wd, gfin):
    row = lambda i: (i, 0)
    return pl.pallas_call(
        _ffn_body,
        grid=(N_TOK // TM,),
        in_specs=[
            pl.BlockSpec((TM, D_MODEL), row),
            _const_spec((1, D_MODEL)),
            _const_spec((D_MODEL, D_FF)),
            _const_spec((D_MODEL, D_FF)),
            _const_spec((D_FF, D_MODEL)),
            _const_spec((1, D_MODEL)),
        ],
        out_specs=pl.BlockSpec((TM, D_MODEL), row),
        out_shape=jax.ShapeDtypeStruct((N_TOK, D_MODEL), f32),
        compiler_params=_cparams(("arbitrary",)),
        name="p_ffn",
    )(hp, gf, wg, wu, wd, gfin)


def _s_inproj_body(x_ref, g_ref, wu_ref, wv_ref, wz_ref, wx_ref, wdt_ref, lng_ref, lnb_ref,
                   sc_ref, cw_ref, cb_ref, dtb_ref, a_ref, w00_ref, b0_ref, e64_ref,
                   gout_ref, v_ref, z_ref, conv_ref, xs_ref, xdt_ref, bc_ref, dec_ref):
    gu, v, z, xbc, dtr = _in_proj(x_ref[...], g_ref[...], wu_ref[...], wv_ref[...], wz_ref[...],
                                  wx_ref[...], wdt_ref[...], lng_ref[...], lnb_ref[...])
    v_ref[...] = v
    z_ref[...] = z
    gout_ref[...] = gu * (v * w00_ref[...] + b0_ref[...])

    acc = cb_ref[...]
    for k in range(CONV_W - 1):
        acc = acc + sc_ref[:, k * CONV_DIM:(k + 1) * CONV_DIM] * cw_ref[k:k + 1, :]
    acc = acc + xbc * cw_ref[CONV_W - 1:CONV_W, :]
    xc = _silu(acc)
    conv_ref[:, 0:(CONV_W - 2) * CONV_DIM] = sc_ref[:, CONV_DIM:(CONV_W - 1) * CONV_DIM]
    conv_ref[:, (CONV_W - 2) * CONV_DIM:] = xbc

    xs = xc[:, :D_SSD]
    dt = _softplus(dtr + dtb_ref[...])
    dec_ref[...] = jnp.exp(dt * a_ref[...])
    xs_ref[...] = xs
    xdt_ref[...] = xs * _select_dot(dt, e64_ref[...], 2)
    bc_ref[...] = xc[:, D_SSD:]


def _s_inproj(x, g, wu, wv, wz, wx, wdt, lng, lnb, sc, cw, cb, dtb, a_neg, w00, b0, e64):
    nb = DEC_BATCH
    args = (x, g, wu, wv, wz, wx, wdt, lng, lnb, sc, cw, cb, dtb, a_neg, w00, b0, e64)
    return pl.pallas_call(
        _s_inproj_body,
        grid=(1,),
        in_specs=[_const_spec(a.shape) for a in args],
        out_specs=[
            _whole_spec((nb, D_GMLP)), _whole_spec((nb, D_GMLP)), _whole_spec((nb, D_SSD)),
            _whole_spec((nb, (CONV_W - 1) * CONV_DIM)), _whole_spec((nb, D_SSD)),
            _whole_spec((nb, D_SSD)), _whole_spec((nb, 2 * SSD_GROUPS * D_STATE)),
            _whole_spec((nb, LANES)),
        ],
        out_shape=[
            jax.ShapeDtypeStruct((nb, D_GMLP), f32), jax.ShapeDtypeStruct((nb, D_GMLP), f32),
            jax.ShapeDtypeStruct((nb, D_SSD), f32),
            jax.ShapeDtypeStruct((nb, (CONV_W - 1) * CONV_DIM), f32),
            jax.ShapeDtypeStruct((nb, D_SSD), f32), jax.ShapeDtypeStruct((nb, D_SSD), f32),
            jax.ShapeDtypeStruct((nb, 2 * SSD_GROUPS * D_STATE), f32),
            jax.ShapeDtypeStruct((nb, LANES), f32),
        ],
        compiler_params=_cparams(("arbitrary",)),
        name="s_inproj",
    )(*args)


def _s_state_body(dec_ref, h0_ref, xdt_ref, bc_ref, h_ref, y_ref):
    step = pl.program_id(0)
    gw = D_SSD // SSD_GROUPS
    hpg = SSD_HEADS // SSD_GROUPS
    nbc = SSD_GROUPS * D_STATE
    xdt = xdt_ref[...]
    bc = bc_ref[...]
    row = lax.broadcasted_iota(jnp.int32, (SB, D_STATE), 0)
    y = [jnp.zeros((SB, gw), f32) for _ in range(SSD_GROUPS)]
    for i in range(SB):
        only_i = row == i
        for g in range(SSD_GROUPS):
            b_i = jnp.where(only_i, bc[:, g * D_STATE:(g + 1) * D_STATE], 0.0)
            c_i = jnp.where(only_i, bc[:, nbc + g * D_STATE:nbc + (g + 1) * D_STATE], 0.0)
            upd = _dot_tn(xdt[:, g * gw:(g + 1) * gw], b_i)
            blocks = []
            for j in range(hpg):
                h = g * hpg + j
                rows = slice(h * SSD_HEAD_DIM, (h + 1) * SSD_HEAD_DIM)
                d = dec_ref[(step * SB + i) * SSD_HEADS + h]
                blocks.append(h0_ref[i, rows, :] * d + upd[j * SSD_HEAD_DIM:(j + 1) * SSD_HEAD_DIM, :])
            h_new = jnp.concatenate(blocks, axis=0)
            h_ref[i, g * gw:(g + 1) * gw, :] = h_new
            y[g] = y[g] + _dot_nt(c_i.astype(bf16), h_new.astype(bf16))
    y_ref[...] = jnp.concatenate(y, axis=1)


def _s_state(dec, h0, xdt, bc):
    blk = lambda i: (i, 0, 0)
    row = lambda i: (i, 0)
    return pl.pallas_call(
        _s_state_body,
        grid=(DEC_BATCH // SB,),
        in_specs=[
            pl.BlockSpec(memory_space=pltpu.SMEM),
            pl.BlockSpec((SB, D_SSD, D_STATE), blk),
            pl.BlockSpec((SB, D_SSD), row),
            pl.BlockSpec((SB, 2 * SSD_GROUPS * D_STATE), row),
        ],
        out_specs=[pl.BlockSpec((SB, D_SSD, D_STATE), blk), pl.BlockSpec((SB, D_SSD), row)],
        out_shape=[jax.ShapeDtypeStruct((DEC_BATCH, D_SSD, D_STATE), f32),
                   jax.ShapeDtypeStruct((DEC_BATCH, D_SSD), f32)],
        compiler_params=_cparams(("arbitrary",)),
        name="s_state",
    )(dec, h0, xdt, bc)


def _s_mixout_body(x_ref, gout_ref, y_ref, xs_ref, z_ref, dsk_ref, sn_ref, wo_ref, gx_ref, wq_ref,
                   hs_ref, q_ref):
    y = y_ref[...] + dsk_ref[...] * xs_ref[...]
    s_out = _gated_group_norm(y, z_ref[...], sn_ref[...])
    mix = jnp.concatenate([gout_ref[...], s_out], axis=1).astype(bf16)
    hs = x_ref[...] + _dot(mix, wo_ref[...])
    hs_ref[...] = hs
    q_ref[...] = _dot(_rmsnorm(hs, gx_ref[...]).astype(bf16), wq_ref[...])


def _s_mixout(x, gout, y, xs, z, dsk, sn, wo, gx, wq):
    args = (x, gout, y, xs, z, dsk, sn, wo, gx, wq)
    return pl.pallas_call(
        _s_mixout_body,
        grid=(1,),
        in_specs=[_const_spec(a.shape) for a in args],
        out_specs=[_whole_spec((DEC_BATCH, D_MODEL)), _whole_spec((DEC_BATCH, D_MODEL))],
        out_shape=[jax.ShapeDtypeStruct((DEC_BATCH, D_MODEL), f32),
                   jax.ShapeDtypeStruct((DEC_BATCH, D_MODEL), f32)],
        compiler_params=_cparams(("arbitrary",)),
        name="s_mixout",
    )(*args)


def _s_attn_body(q_ref, k_ref, v_ref, o_ref):
    def one(i, carry):
        s = jnp.sum(k_ref[i] * q_ref[i][None], axis=-1, keepdims=True) * (X_HD ** -0.5)
        e = jnp.exp(s - jnp.max(s, axis=0, keepdims=True))
        p = e / jnp.sum(e, axis=0, keepdims=True)
        o_ref[i] = jnp.sum(p * v_ref[i], axis=0)
        return carry

    lax.fori_loop(0, SB, one, 0)


def _s_attn(q, ck, cv):
    blk = lambda i: (i, 0, 0, 0)
    row = lambda i: (i, 0, 0)
    return pl.pallas_call(
        _s_attn_body,
        grid=(DEC_BATCH // SB,),
        in_specs=[
            pl.BlockSpec((SB, X_HEADS, X_HD), row),
            pl.BlockSpec((SB, N_MEM, X_HEADS, X_HD), blk),
            pl.BlockSpec((SB, N_MEM, X_HEADS, X_HD), blk),
        ],
        out_specs=pl.BlockSpec((SB, X_HEADS, X_HD), row),
        out_shape=jax.ShapeDtypeStruct((DEC_BATCH, X_HEADS, X_HD), f32),
        compiler_params=_cparams(("arbitrary",)),
        name="s_attn",
    )(q, ck, cv)


def _s_ffn_body(hs_ref, o_ref, wxo_ref, gf_ref, wg_ref, wu_ref, wd_ref, gfin_ref, y_ref):
    hs = hs_ref[...] + _dot(o_ref[...].astype(bf16), wxo_ref[...])
    y_ref[...] = _ffn(hs, gf_ref[...], wg_ref[...], wu_ref[...], wd_ref[...], gfin_ref[...])


def _s_ffn(hs, o, wxo, gf, wg, wu, wd, gfin):
    args = (hs, o, wxo, gf, wg, wu, wd, gfin)
    return pl.pallas_call(
        _s_ffn_body,
        grid=(1,),
        in_specs=[_const_spec(a.shape) for a in args],
        out_specs=_whole_spec((DEC_BATCH, D_MODEL)),
        out_shape=jax.ShapeDtypeStruct((DEC_BATCH, D_MODEL), f32),
        compiler_params=_cparams(("arbitrary",)),
        name="s_ffn",
    )(*args)


def _tri_ones():
    return jnp.asarray(np.tril(np.ones((CHUNK, CHUNK), np.float32)), bf16)


def _head_select(width):
    e = np.zeros((LANES, SSD_HEADS * width), np.float32)
    for h in range(SSD_HEADS):
        e[h, h * width:(h + 1) * width] = 1.0
    return jnp.asarray(e, bf16)


def _pad_lanes(v):
    return jnp.pad(v.astype(f32), (0, LANES - v.shape[0]))[None, :]


def kernel(x_prompt, x_sample, state_ssm, state_conv, cache_mem_k, cache_mem_v, mem_prompt,
           norm_mix, w_in, gmlp_ln_g, gmlp_ln_b, gmlp_ws, gmlp_bs, conv_w, conv_b, dt_bias,
           a_log, d_skip, ssd_norm, w_out, norm_xattn, norm_mem, w_q, w_k, w_v, w_xo,
           norm_ffn, w_gate, w_up, w_down, norm_final):
    l = 0
    wi = w_in[l].astype(bf16)
    o0, o1, o2, o3 = D_GMLP, 2 * D_GMLP, 2 * D_GMLP + D_SSD, 2 * D_GMLP + D_SSD + CONV_DIM
    wu_in, wv_in, wz_in, wx_in = wi[:, :o0], wi[:, o0:o1], wi[:, o1:o2], wi[:, o2:o3]
    wdt_in = jnp.pad(wi[:, o3:], ((0, 0), (0, LANES - SSD_HEADS)))
    wo = w_out[l].astype(bf16)
    wq, wk, wv, wxo = (w[l].astype(bf16) for w in (w_q, w_k, w_v, w_xo))
    wg, wu, wd = (w[l].astype(bf16) for w in (w_gate, w_up, w_down))

    g_mix, g_x, g_mem, g_ffn = (g[l][None, :] for g in (norm_mix, norm_xattn, norm_mem, norm_ffn))
    g_fin = norm_final[None, :]
    lng, lnb = gmlp_ln_g[l][None, :], gmlp_ln_b[l][None, :]
    cw, cb = conv_w[l], conv_b[l][None, :]
    dtb = _pad_lanes(dt_bias[l])
    a_neg = _pad_lanes(-jnp.exp(a_log[l].astype(f32)))
    dsk = jnp.repeat(d_skip[l].astype(f32), SSD_HEAD_DIM)[None, :]
    sn = ssd_norm[l][None, :]
    ws = gmlp_ws[l]
    bsx = jnp.repeat(jnp.swapaxes(gmlp_bs[l], 0, 1), GMLP_HD, axis=1)
    w00 = jnp.repeat(ws[:, 0, 0], GMLP_HD)[None, :]
    b0 = bsx[0:1, :]
    tri, e64, e128 = _tri_ones(), _head_select(SSD_HEAD_DIM), _head_select(CHUNK)

    x2 = x_prompt.reshape(N_TOK, D_MODEL)
    gu, v, gv_p, z, xbc, dtr = _p_inproj(x2, g_mix, wu_in, wv_in, wz_in, wx_in, wdt_in, lng, lnb)
    mix, ssm_p, conv8 = _p_mix(gu, v, z, xbc, dtr, ws, bsx, cw, cb, dtb, a_neg, dsk, sn, tri, e64, e128)
    mk, mv, mkb, mvb = _memkv(mem_prompt.reshape(BATCH * N_MEM, D_MODEL), g_mem, wk, wv)
    hp = _p_attn(x2, mix, wo, g_x, wq, mkb.reshape(BATCH, N_MEM, D_MODEL),
                 mvb.reshape(BATCH, N_MEM, D_MODEL), wxo)
    y_prompt = _p_ffn(hp, g_ffn, wg, wu, wd, g_fin).reshape(BATCH, SEQ, D_MODEL)

    xs2 = x_sample.reshape(DEC_BATCH, D_MODEL)
    sc = state_conv[l].reshape(DEC_BATCH, (CONV_W - 1) * CONV_DIM)
    gout, v_s, z_s, conv_s, xs_s, xdt_s, bc_s, dec_s = _s_inproj(
        xs2, g_mix, wu_in, wv_in, wz_in, wx_in, wdt_in, lng, lnb, sc, cw, cb, dtb, a_neg, w00, b0, e64)
    h0 = state_ssm[l].reshape(DEC_BATCH, D_SSD, D_STATE)
    ssm_s, y_s = _s_state(dec_s[:, :SSD_HEADS].reshape(-1), h0, xdt_s, bc_s)
    hs, q_s = _s_mixout(xs2, gout, y_s, xs_s, z_s, dsk, sn, wo, g_x, wq)
    o_s = _s_attn(q_s.reshape(DEC_BATCH, X_HEADS, X_HD), cache_mem_k[l], cache_mem_v[l])
    o_s = o_s.reshape(DEC_BATCH, D_MODEL)
    y_sample = _s_ffn(hs, o_s, wxo, g_ffn, wg, wu, wd, g_fin).reshape(DEC_BATCH, 1, D_MODEL)

    return (
        y_prompt,
        y_sample,
        ssm_p.reshape(1, BATCH, SSD_HEADS, SSD_HEAD_DIM, D_STATE),
        conv8[:, SUBLANES - (CONV_W - 1):, :][None],
        gv_p[None],
        mk.reshape(1, BATCH, N_MEM, X_HEADS, X_HD),
        mv.reshape(1, BATCH, N_MEM, X_HEADS, X_HD),
        ssm_s.reshape(1, DEC_BATCH, SSD_HEADS, SSD_HEAD_DIM, D_STATE),
        conv_s.reshape(1, DEC_BATCH, CONV_W - 1, CONV_DIM),
        v_s.reshape(1, DEC_BATCH, 1, D_GMLP),
    )
```

```python
import numpy as np
import jax
import jax.numpy as jnp
from jax import lax
from jax.experimental import pallas as pl
from jax.experimental.pallas import tpu as pltpu

f32 = jnp.float32
bf16 = jnp.bfloat16

D_MODEL = 1024
BATCH = 8
SEQ = 2048
DEC_BATCH = 128
D_GMLP = 512
GMLP_HEADS = 4
GMLP_HD = 128
CHUNK = 128
D_SSD = 512
SSD_HEAD_DIM = 64
SSD_HEADS = 8
SSD_GROUPS = 2
D_STATE = 128
CONV_W = 4
CONV_DIM = 1024
N_MEM = 256
X_HEADS = 4
X_HD = 256
D_FF = 2816
EPS = 1e-6

N_TOK = BATCH * SEQ
TM = 512
LANES = 128
SUBLANES = 8
VMEM_LIMIT = 56 * 1024 * 1024
SB = 8
NCH = 4
TMIX = NCH * CHUNK


def _cparams(sem):
    return pltpu.CompilerParams(dimension_semantics=sem, vmem_limit_bytes=VMEM_LIMIT)


def _const_spec(shape):
    nd = len(shape)
    return pl.BlockSpec(shape, lambda *_: (0,) * nd, pipeline_mode=pl.Buffered(1))


def _whole_spec(shape):
    nd = len(shape)
    return pl.BlockSpec(shape, lambda *_: (0,) * nd)


def _dot(a, b):
    return jnp.dot(a, b, preferred_element_type=f32)


def _dot_nt(a, b):
    return lax.dot_general(a, b, (((1,), (1,)), ((), ())), preferred_element_type=f32)


def _dot_tn(a, b):
    return lax.dot_general(a, b, (((0,), (0,)), ((), ())), preferred_element_type=f32)


def _rmsnorm(x, g):
    return x * lax.rsqrt(jnp.mean(x * x, axis=-1, keepdims=True) + EPS) * g


def _layernorm(x, g, b):
    mu = jnp.mean(x, axis=-1, keepdims=True)
    xc = x - mu
    return xc * lax.rsqrt(jnp.mean(xc * xc, axis=-1, keepdims=True) + EPS) * g + b


LOG2E = 1.4426950408889634
GELU_C = 0.7978845608028654
GELU_A = 0.044715


def _sigmoid(x):
    return 1.0 / (1.0 + jnp.exp2(x * (-LOG2E)))


def _silu(x):
    return x * _sigmoid(x)


def _gelu_tanh(x):
    k1 = -2.0 * GELU_C * LOG2E
    k2 = k1 * GELU_A
    return x / (1.0 + jnp.exp2(x * (k1 + k2 * (x * x))))


def _softplus(x):
    return jnp.maximum(x, 0.0) + jnp.log1p(jnp.exp(-jnp.abs(x)))


def _exact_parts(x, terms):
    parts = []
    r = x
    for _ in range(terms):
        p = r.astype(bf16).astype(f32)
        parts.append(p)
        r = r - p
    return parts


def _select_dot(x, sel, terms):
    m = x.shape[0]
    stacked = jnp.concatenate(_exact_parts(x, terms), axis=0).astype(bf16)
    r = _dot(stacked, sel)
    out = r[0:m]
    for i in range(1, terms):
        out = out + r[i * m:(i + 1) * m]
    return out


def _gated_group_norm(y, z, sn):
    yz = y * _silu(z)
    half = D_SSD // SSD_GROUPS
    outs = []
    for g in range(SSD_GROUPS):
        blk = yz[:, g * half:(g + 1) * half]
        outs.append(blk * lax.rsqrt(jnp.mean(blk * blk, axis=-1, keepdims=True) + EPS))
    return jnp.concatenate(outs, axis=1) * sn


def _in_proj(xn, wi_ref, wdt, lng, lnb):
    o1, o2, o3 = 2 * D_GMLP, 2 * D_GMLP + D_SSD, 2 * D_GMLP + D_SSD + CONV_DIM
    gu = _gelu_tanh(_dot(xn, wi_ref[:, 0:D_GMLP]))
    v = _layernorm(_gelu_tanh(_dot(xn, wi_ref[:, D_GMLP:o1])), lng, lnb)
    z = _dot(xn, wi_ref[:, o1:o2])
    xbc = _dot(xn, wi_ref[:, o2:o3])
    dtr = _dot(xn, wdt)
    return gu, v, z, xbc, dtr


def _p_mixer_body(x_ref, g_ref, wi_ref, wdt_ref, lng_ref, lnb_ref,
                  ws_ref, bsx_ref, cw_ref, cb_ref, dtb8_ref, a8_ref, dsk_ref, sn_ref,
                  triu_ref, esel_ref, wo_ref,
                  hp_ref, gv_ref, ssm_ref, conv_ref,
                  xp_scr, h_scr):
    j = pl.program_id(1)

    @pl.when(j == 0)
    def _():
        xp_scr[0:SUBLANES, :] = jnp.zeros((SUBLANES, CONV_DIM), f32)
        h_scr[...] = jnp.zeros_like(h_scr)

    t_idx = lax.broadcasted_iota(jnp.int32, (CHUNK, CHUNK), 0)
    s_idx = lax.broadcasted_iota(jnp.int32, (CHUNK, CHUNK), 1)
    causal = t_idx >= s_idx
    w_tril = [jnp.where(causal, ws_ref[hh], 0.0).astype(bf16) for hh in range(GMLP_HEADS)]
    gw = D_SSD // SSD_GROUPS
    hpg = SSD_HEADS // SSD_GROUPS
    zeros_pad = jnp.zeros((CHUNK - 3 * SSD_HEADS, CHUNK), f32)

    xn_all = _rmsnorm(x_ref[...], g_ref[...]).astype(bf16)

    def project(ci):
        return _in_proj(xn_all[ci * CHUNK:(ci + 1) * CHUNK, :], wi_ref, wdt_ref[...],
                        lng_ref[...], lnb_ref[...])

    nxt = project(0)
    for ci in range(NCH):
        rows = slice(ci * CHUNK, (ci + 1) * CHUNK)
        gu, v, z, xbc, dtr = nxt
        if ci + 1 < NCH:
            nxt = project(ci + 1)

        xp_scr[SUBLANES:SUBLANES + CHUNK, :] = xbc
        acc = cb_ref[...]
        for k in range(CONV_W - 1):
            acc = acc + xp_scr[pl.ds(SUBLANES - (CONV_W - 1) + k, CHUNK), :] * cw_ref[k:k + 1, :]
        acc = acc + xbc * cw_ref[CONV_W - 1:CONV_W, :]
        xc = _silu(acc)
        tail = xbc[CHUNK - SUBLANES:, :]
        xp_scr[0:SUBLANES, :] = tail

        xs = xc[:, :D_SSD]
        b_all = xc[:, D_SSD:D_SSD + SSD_GROUPS * D_STATE].astype(bf16)
        c_all = xc[:, D_SSD + SSD_GROUPS * D_STATE:].astype(bf16)

        dt8 = _softplus(dtr.T[0:SSD_HEADS, :] + dtb8_ref[...])
        a8 = dt8 * a8_ref[...]
        a_parts = jnp.concatenate(_exact_parts(a8, 3) + [zeros_pad], axis=0).astype(bf16)
        a_sum = _dot(a_parts, triu_ref[...])
        acum8 = a_sum[0:8] + a_sum[8:16] + a_sum[16:24]
        a_last8 = acum8[:, CHUNK - 1:CHUNK]
        expa8 = jnp.exp(acum8)
        wgt8 = jnp.exp(a_last8 - acum8) * dt8
        cd8 = jnp.exp(a_last8)

        stacked = jnp.concatenate([dt8, expa8, wgt8, acum8], axis=0)
        lhs = jnp.concatenate(_exact_parts(stacked, 3) + [jnp.zeros((32, CHUNK), f32)], axis=0)
        bc = _dot(lhs.T.astype(bf16), esel_ref[...])
        dt_even = bc[:, 0:D_SSD]
        dt_odd = bc[:, D_SSD:2 * D_SSD]
        expa_e = bc[:, 2 * D_SSD:3 * D_SSD]
        wgt_e = bc[:, 3 * D_SSD:4 * D_SSD]
        col_b = bc[:, 4 * D_SSD:]

        cb = []
        for g in range(SSD_GROUPS):
            sl = slice(g * D_STATE, (g + 1) * D_STATE)
            cb.append(_dot_nt(c_all[:, sl], b_all[:, sl]))

        xdt_lo = xs * dt_even
        xdt_hi = xs * dt_odd
        y_diag = []
        for q in range(SSD_HEADS // 2):
            g = (2 * q) // hpg
            w_pair = []
            for h in (2 * q, 2 * q + 1):
                seg = col_b[:, h * CHUNK:(h + 1) * CHUNK] - acum8[h:h + 1, :]
                decay = jnp.exp(jnp.where(causal, seg, -jnp.inf))
                w_pair.append((cb[g] * decay).astype(bf16))
            sl = slice(q * LANES, (q + 1) * LANES)
            rhs = jnp.concatenate([xdt_lo[:, sl], xdt_hi[:, sl]], axis=0).astype(bf16)
            y_diag.append(_dot(jnp.concatenate(w_pair, axis=1), rhs))
        y_diag = jnp.concatenate(y_diag, axis=1)

        h_prev = h_scr[...].astype(bf16)
        y_off = jnp.concatenate(
            [_dot_nt(c_all[:, g * D_STATE:(g + 1) * D_STATE], h_prev[g * gw:(g + 1) * gw, :])
             for g in range(SSD_GROUPS)], axis=1) * expa_e

        xw = (xs * wgt_e).astype(bf16)
        states = [_dot_tn(xw[:, g * gw:(g + 1) * gw], b_all[:, g * D_STATE:(g + 1) * D_STATE])
                  for g in range(SSD_GROUPS)]
        for h in range(SSD_HEADS):
            g, jh = h // hpg, h % hpg
            hr = slice(h * SSD_HEAD_DIM, (h + 1) * SSD_HEAD_DIM)
            h_scr[hr, :] = (h_scr[hr, :] * cd8[h:h + 1, :]
                            + states[g][jh * SSD_HEAD_DIM:(jh + 1) * SSD_HEAD_DIM, :])

        y = y_diag + y_off
        y = y + dsk_ref[...] * xs
        s_out = _gated_group_norm(y, z, sn_ref[...])

        vb = v.astype(bf16)
        mixed = jnp.concatenate(
            [_dot(w_tril[hh], vb[:, hh * GMLP_HD:(hh + 1) * GMLP_HD]) for hh in range(GMLP_HEADS)],
            axis=1) + bsx_ref[...]
        g_out = gu * mixed

        mix = jnp.concatenate([g_out, s_out], axis=1).astype(bf16)
        hp_ref[rows, :] = x_ref[rows, :] + _dot(mix, wo_ref[...])

    gv_ref[0] = v
    conv_ref[0] = tail

    @pl.when(j == SEQ // TMIX - 1)
    def _():
        ssm_ref[0] = h_scr[...]


def _p_mixer(x2, g, wi, wdt, lng, lnb, ws, bsx, cw, cb, dtb8, a8, dsk, sn, triu, esel, wo):
    tiles = SEQ // TMIX
    row = lambda b, j: (b * tiles + j, 0)
    per_b = lambda b, j: (b, 0, 0)
    consts = (g, wi, wdt, lng, lnb, ws, bsx, cw, cb, dtb8, a8, dsk, sn, triu, esel, wo)
    return pl.pallas_call(
        _p_mixer_body,
        grid=(BATCH, tiles),
        in_specs=[pl.BlockSpec((TMIX, D_MODEL), row)] + [_const_spec(a.shape) for a in consts],
        out_specs=[
            pl.BlockSpec((TMIX, D_MODEL), row),
            pl.BlockSpec((1, CHUNK, D_GMLP), per_b),
            pl.BlockSpec((1, D_SSD, D_STATE), per_b),
            pl.BlockSpec((1, SUBLANES, CONV_DIM), per_b),
        ],
        out_shape=[
            jax.ShapeDtypeStruct((N_TOK, D_MODEL), f32),
            jax.ShapeDtypeStruct((BATCH, CHUNK, D_GMLP), f32),
            jax.ShapeDtypeStruct((BATCH, D_SSD, D_STATE), f32),
            jax.ShapeDtypeStruct((BATCH, SUBLANES, CONV_DIM), f32),
        ],
        scratch_shapes=[
            pltpu.VMEM((SUBLANES + CHUNK, CONV_DIM), f32),
            pltpu.VMEM((D_SSD, D_STATE), f32),
        ],
        compiler_params=_cparams(("arbitrary", "arbitrary")),
        name="p_mixer",
    )(x2, *consts)


def _memkv_body(m_ref, g_ref, wk_ref, wv_ref, k4_ref, v4_ref, kb_ref, vb_ref):
    mn = _rmsnorm(m_ref[...], g_ref[...]).astype(bf16)
    for w_ref, o4_ref, ob_ref in ((wk_ref, k4_ref, kb_ref), (wv_ref, v4_ref, vb_ref)):
        r = _dot(mn, w_ref[...])
        ob_ref[...] = r.astype(bf16)
        for h in range(X_HEADS):
            o4_ref[:, h, :] = r[:, h * X_HD:(h + 1) * X_HD]


def _memkv(mem2, g, wk, wv):
    n = mem2.shape[0]
    row = lambda i: (i, 0)
    row4 = lambda i: (i, 0, 0)
    spec4 = pl.BlockSpec((TM, X_HEADS, X_HD), row4)
    spec2 = pl.BlockSpec((TM, D_MODEL), row)
    shape4 = jax.ShapeDtypeStruct((n, X_HEADS, X_HD), f32)
    shape2 = jax.ShapeDtypeStruct((n, D_MODEL), bf16)
    return pl.pallas_call(
        _memkv_body,
        grid=(n // TM,),
        in_specs=[
            spec2,
            _const_spec((1, D_MODEL)),
            _const_spec((D_MODEL, D_MODEL)),
            _const_spec((D_MODEL, D_MODEL)),
        ],
        out_specs=[spec4, spec4, spec2, spec2],
        out_shape=[shape4, shape4, shape2, shape2],
        compiler_params=_cparams(("arbitrary",)),
        name="memkv",
    )(mem2, g, wk, wv)


def _p_attn_body(hp_ref, gx_ref, wq_ref, k_ref, v_ref, wxo_ref, out_ref):
    hp = hp_ref[...]
    hn = _rmsnorm(hp, gx_ref[...]).astype(bf16)
    q = _dot(hn, wq_ref[...]).astype(bf16)
    kb = k_ref[0]
    vb = v_ref[0]
    outs = []
    for h in range(X_HEADS):
        sl = slice(h * X_HD, (h + 1) * X_HD)
        s = _dot_nt(q[:, sl], kb[:, sl]) * (X_HD ** -0.5)
        e = jnp.exp(s - jnp.max(s, axis=-1, keepdims=True))
        p = e / jnp.sum(e, axis=-1, keepdims=True)
        outs.append(_dot(p.astype(bf16), vb[:, sl]))
    o = jnp.concatenate(outs, axis=1).astype(bf16)
    out_ref[...] = hp + _dot(o, wxo_ref[...])


def _p_attn(hp, gx, wq, mk, mv, wxo):
    tiles = SEQ // TM
    row = lambda b, j: (b * tiles + j, 0)
    per_b = lambda b, j: (b, 0, 0)
    return pl.pallas_call(
        _p_attn_body,
        grid=(BATCH, tiles),
        in_specs=[
            pl.BlockSpec((TM, D_MODEL), row),
            _const_spec((1, D_MODEL)),
            _const_spec((D_MODEL, D_MODEL)),
            pl.BlockSpec((1, N_MEM, D_MODEL), per_b),
            pl.BlockSpec((1, N_MEM, D_MODEL), per_b),
            _const_spec((D_MODEL, D_MODEL)),
        ],
        out_specs=pl.BlockSpec((TM, D_MODEL), row),
        out_shape=jax.ShapeDtypeStruct((N_TOK, D_MODEL), f32),
        compiler_params=_cparams(("arbitrary", "arbitrary")),
        name="p_attn",
    )(hp, gx, wq, mk, mv, wxo)


def _ffn(hp, gf, wg, wu, wd, gfin):
    hn = _rmsnorm(hp, gf).astype(bf16)
    hid = (_silu(_dot(hn, wg)) * _dot(hn, wu)).astype(bf16)
    return _rmsnorm(hp + _dot(hid, wd), gfin)


def _ffn_body(h_ref, gf_ref, wg_ref, wu_ref, wd_ref, gfin_ref, y_ref):
    y_ref[...] = _ffn(h_ref[...], gf_ref[...], wg_ref[...], wu_ref[...], wd_ref[...], gfin_ref[...])


def _p_ffn(hp, gf, wg, wu, wd, gfin):
    row = lambda i: (i, 0)
    return pl.pallas_call(
        _ffn_body,
        grid=(N_TOK // TM,),
        in_specs=[
            pl.BlockSpec((TM, D_MODEL), row),
            _const_spec((1, D_MODEL)),
            _const_spec((D_MODEL, D_FF)),
            _const_spec((D_MODEL, D_FF)),
            _const_spec((D_FF, D_MODEL)),
            _const_spec((1, D_MODEL)),
        ],
        out_specs=pl.BlockSpec((TM, D_MODEL), row),
        out_shape=jax.ShapeDtypeStruct((N_TOK, D_MODEL), f32),
        compiler_params=_cparams(("arbitrary",)),
        name="p_ffn",
    )(hp, gf, wg, wu, wd, gfin)


def _s_inproj_body(x_ref, g_ref, wi_ref, wdt_ref, lng_ref, lnb_ref,
                   sc_ref, cw_ref, cb_ref, dtb_ref, a_ref, w00_ref, b0_ref, e64_ref,
                   gout_ref, v_ref, z_ref, conv_ref, xs_ref, xdt_ref, bc_ref, dec_ref):
    xn = _rmsnorm(x_ref[...], g_ref[...]).astype(bf16)
    gu, v, z, xbc, dtr = _in_proj(xn, wi_ref, wdt_ref[...], lng_ref[...], lnb_ref[...])
    v_ref[...] = v
    z_ref[...] = z
    gout_ref[...] = gu * (v * w00_ref[...] + b0_ref[...])

    acc = cb_ref[...]
    for k in range(CONV_W - 1):
        acc = acc + sc_ref[:, k * CONV_DIM:(k + 1) * CONV_DIM] * cw_ref[k:k + 1, :]
    acc = acc + xbc * cw_ref[CONV_W - 1:CONV_W, :]
    xc = _silu(acc)
    conv_ref[:, 0:(CONV_W - 2) * CONV_DIM] = sc_ref[:, CONV_DIM:(CONV_W - 1) * CONV_DIM]
    conv_ref[:, (CONV_W - 2) * CONV_DIM:] = xbc

    xs = xc[:, :D_SSD]
    dt = _softplus(dtr + dtb_ref[...])
    dec_ref[...] = jnp.exp(dt * a_ref[...])
    xs_ref[...] = xs
    xdt_ref[...] = xs * _select_dot(dt, e64_ref[...], 2)
    bc_ref[...] = xc[:, D_SSD:]


def _s_inproj(x, g, wi, wdt, lng, lnb, sc, cw, cb, dtb, a_neg, w00, b0, e64):
    nb = DEC_BATCH
    args = (x, g, wi, wdt, lng, lnb, sc, cw, cb, dtb, a_neg, w00, b0, e64)
    return pl.pallas_call(
        _s_inproj_body,
        grid=(1,),
        in_specs=[_const_spec(a.shape) for a in args],
        out_specs=[
            _whole_spec((nb, D_GMLP)), _whole_spec((nb, D_GMLP)), _whole_spec((nb, D_SSD)),
            _whole_spec((nb, (CONV_W - 1) * CONV_DIM)), _whole_spec((nb, D_SSD)),
            _whole_spec((nb, D_SSD)), _whole_spec((nb, 2 * SSD_GROUPS * D_STATE)),
            _whole_spec((nb, LANES)),
        ],
        out_shape=[
            jax.ShapeDtypeStruct((nb, D_GMLP), f32), jax.ShapeDtypeStruct((nb, D_GMLP), f32),
            jax.ShapeDtypeStruct((nb, D_SSD), f32),
            jax.ShapeDtypeStruct((nb, (CONV_W - 1) * CONV_DIM), f32),
            jax.ShapeDtypeStruct((nb, D_SSD), f32), jax.ShapeDtypeStruct((nb, D_SSD), f32),
            jax.ShapeDtypeStruct((nb, 2 * SSD_GROUPS * D_STATE), f32),
            jax.ShapeDtypeStruct((nb, LANES), f32),
        ],
        compiler_params=_cparams(("arbitrary",)),
        name="s_inproj",
    )(*args)


def _s_state_body(dec_ref, h0_ref, xdt_ref, bc_ref, h_ref, y_ref):
    step = pl.program_id(0)
    gw = D_SSD // SSD_GROUPS
    hpg = SSD_HEADS // SSD_GROUPS
    nbc = SSD_GROUPS * D_STATE
    xdt = xdt_ref[...]
    bc = bc_ref[...]
    row = lax.broadcasted_iota(jnp.int32, (SB, D_STATE), 0)
    y = [jnp.zeros((SB, gw), f32) for _ in range(SSD_GROUPS)]
    for i in range(SB):
        only_i = row == i
        for g in range(SSD_GROUPS):
            b_i = jnp.where(only_i, bc[:, g * D_STATE:(g + 1) * D_STATE], 0.0)
            c_i = jnp.where(only_i, bc[:, nbc + g * D_STATE:nbc + (g + 1) * D_STATE], 0.0)
            upd = _dot_tn(xdt[:, g * gw:(g + 1) * gw], b_i)
            blocks = []
            for j in range(hpg):
                h = g * hpg + j
                rows = slice(h * SSD_HEAD_DIM, (h + 1) * SSD_HEAD_DIM)
                d = dec_ref[(step * SB + i) * SSD_HEADS + h]
                blocks.append(h0_ref[i, rows, :] * d + upd[j * SSD_HEAD_DIM:(j + 1) * SSD_HEAD_DIM, :])
            h_new = jnp.concatenate(blocks, axis=0)
            h_ref[i, g * gw:(g + 1) * gw, :] = h_new
            y[g] = y[g] + _dot_nt(c_i.astype(bf16), h_new.astype(bf16))
    y_ref[...] = jnp.concatenate(y, axis=1)


def _s_state(dec, h0, xdt, bc):
    blk = lambda i: (i, 0, 0)
    row = lambda i: (i, 0)
    return pl.pallas_call(
        _s_state_body,
        grid=(DEC_BATCH // SB,),
        in_specs=[
            pl.BlockSpec(memory_space=pltpu.SMEM),
            pl.BlockSpec((SB, D_SSD, D_STATE), blk),
            pl.BlockSpec((SB, D_SSD), row),
            pl.BlockSpec((SB, 2 * SSD_GROUPS * D_STATE), row),
        ],
        out_specs=[pl.BlockSpec((SB, D_SSD, D_STATE), blk), pl.BlockSpec((SB, D_SSD), row)],
        out_shape=[jax.ShapeDtypeStruct((DEC_BATCH, D_SSD, D_STATE), f32),
                   jax.ShapeDtypeStruct((DEC_BATCH, D_SSD), f32)],
        compiler_params=_cparams(("arbitrary",)),
        name="s_state",
    )(dec, h0, xdt, bc)


def _s_mixout_body(x_ref, gout_ref, y_ref, xs_ref, z_ref, dsk_ref, sn_ref, wo_ref, gx_ref, wq_ref,
                   hs_ref, q_ref):
    y = y_ref[...] + dsk_ref[...] * xs_ref[...]
    s_out = _gated_group_norm(y, z_ref[...], sn_ref[...])
    mix = jnp.concatenate([gout_ref[...], s_out], axis=1).astype(bf16)
    hs = x_ref[...] + _dot(mix, wo_ref[...])
    hs_ref[...] = hs
    q_ref[...] = _dot(_rmsnorm(hs, gx_ref[...]).astype(bf16), wq_ref[...])


def _s_mixout(x, gout, y, xs, z, dsk, sn, wo, gx, wq):
    args = (x, gout, y, xs, z, dsk, sn, wo, gx, wq)
    return pl.pallas_call(
        _s_mixout_body,
        grid=(1,),
        in_specs=[_const_spec(a.shape) for a in args],
        out_specs=[_whole_spec((DEC_BATCH, D_MODEL)), _whole_spec((DEC_BATCH, D_MODEL))],
        out_shape=[jax.ShapeDtypeStruct((DEC_BATCH, D_MODEL), f32),
                   jax.ShapeDtypeStruct((DEC_BATCH, D_MODEL), f32)],
        compiler_params=_cparams(("arbitrary",)),
        name="s_mixout",
    )(*args)


def _s_attn_body(q_ref, k_ref, v_ref, o_ref):
    def one(i, carry):
        s = jnp.sum(k_ref[i] * q_ref[i][None], axis=-1, keepdims=True) * (X_HD ** -0.5)
        e = jnp.exp(s - jnp.max(s, axis=0, keepdims=True))
        p = e / jnp.sum(e, axis=0, keepdims=True)
        o_ref[i] = jnp.sum(p * v_ref[i], axis=0)
        return carry

    lax.fori_loop(0, SB, one, 0)


def _s_attn(q, ck, cv):
    blk = lambda i: (i, 0, 0, 0)
    row = lambda i: (i, 0, 0)
    return pl.pallas_call(
        _s_attn_body,
        grid=(DEC_BATCH // SB,),
        in_specs=[
            pl.BlockSpec((SB, X_HEADS, X_HD), row),
            pl.BlockSpec((SB, N_MEM, X_HEADS, X_HD), blk),
            pl.BlockSpec((SB, N_MEM, X_HEADS, X_HD), blk),
        ],
        out_specs=pl.BlockSpec((SB, X_HEADS, X_HD), row),
        out_shape=jax.ShapeDtypeStruct((DEC_BATCH, X_HEADS, X_HD), f32),
        compiler_params=_cparams(("arbitrary",)),
        name="s_attn",
    )(q, ck, cv)


def _s_ffn_body(hs_ref, o_ref, wxo_ref, gf_ref, wg_ref, wu_ref, wd_ref, gfin_ref, y_ref):
    hs = hs_ref[...] + _dot(o_ref[...].astype(bf16), wxo_ref[...])
    y_ref[...] = _ffn(hs, gf_ref[...], wg_ref[...], wu_ref[...], wd_ref[...], gfin_ref[...])


def _s_ffn(hs, o, wxo, gf, wg, wu, wd, gfin):
    args = (hs, o, wxo, gf, wg, wu, wd, gfin)
    return pl.pallas_call(
        _s_ffn_body,
        grid=(1,),
        in_specs=[_const_spec(a.shape) for a in args],
        out_specs=_whole_spec((DEC_BATCH, D_MODEL)),
        out_shape=jax.ShapeDtypeStruct((DEC_BATCH, D_MODEL), f32),
        compiler_params=_cparams(("arbitrary",)),
        name="s_ffn",
    )(*args)


def _triu_ones():
    return jnp.asarray(np.triu(np.ones((CHUNK, CHUNK), np.float32)), bf16)


def _chunk_select():
    e = np.zeros((LANES, 4 * D_SSD + SSD_HEADS * CHUNK), np.float32)
    for part in range(3):
        for h in range(SSD_HEADS):
            lanes = slice(h * SSD_HEAD_DIM, (h + 1) * SSD_HEAD_DIM)
            r = part * 32 + h
            e[r, (h % 2) * D_SSD:(h % 2 + 1) * D_SSD][lanes] = 1.0
            e[r + SSD_HEADS, 2 * D_SSD:3 * D_SSD][lanes] = 1.0
            e[r + 2 * SSD_HEADS, 3 * D_SSD:4 * D_SSD][lanes] = 1.0
            e[r + 3 * SSD_HEADS, 4 * D_SSD + h * CHUNK:4 * D_SSD + (h + 1) * CHUNK] = 1.0
    return jnp.asarray(e, bf16)


def _head_select(width):
    e = np.zeros((LANES, SSD_HEADS * width), np.float32)
    for h in range(SSD_HEADS):
        e[h, h * width:(h + 1) * width] = 1.0
    return jnp.asarray(e, bf16)


def _pad_lanes(v):
    return jnp.pad(v.astype(f32), (0, LANES - v.shape[0]))[None, :]


def kernel(x_prompt, x_sample, state_ssm, state_conv, cache_mem_k, cache_mem_v, mem_prompt,
           norm_mix, w_in, gmlp_ln_g, gmlp_ln_b, gmlp_ws, gmlp_bs, conv_w, conv_b, dt_bias,
           a_log, d_skip, ssd_norm, w_out, norm_xattn, norm_mem, w_q, w_k, w_v, w_xo,
           norm_ffn, w_gate, w_up, w_down, norm_final):
    l = 0
    wi = w_in[l].astype(bf16)
    o3 = 2 * D_GMLP + D_SSD + CONV_DIM
    wdt_in = jnp.pad(wi[:, o3:], ((0, 0), (0, LANES - SSD_HEADS)))
    wo = w_out[l].astype(bf16)
    wq, wk, wv, wxo = (w[l].astype(bf16) for w in (w_q, w_k, w_v, w_xo))
    wg, wu, wd = (w[l].astype(bf16) for w in (w_gate, w_up, w_down))

    g_mix, g_x, g_mem, g_ffn = (g[l][None, :] for g in (norm_mix, norm_xattn, norm_mem, norm_ffn))
    g_fin = norm_final[None, :]
    lng, lnb = gmlp_ln_g[l][None, :], gmlp_ln_b[l][None, :]
    cw, cb = conv_w[l], conv_b[l][None, :]
    a_vec = -jnp.exp(a_log[l].astype(f32))
    dtb = _pad_lanes(dt_bias[l])
    a_neg = _pad_lanes(a_vec)
    dtb8 = jnp.broadcast_to(dt_bias[l].astype(f32)[:, None], (SSD_HEADS, CHUNK))
    a8 = jnp.broadcast_to(a_vec[:, None], (SSD_HEADS, CHUNK))
    dsk = jnp.repeat(d_skip[l].astype(f32), SSD_HEAD_DIM)[None, :]
    sn = ssd_norm[l][None, :]
    ws = gmlp_ws[l]
    bsx = jnp.repeat(jnp.swapaxes(gmlp_bs[l], 0, 1), GMLP_HD, axis=1)
    w00 = jnp.repeat(ws[:, 0, 0], GMLP_HD)[None, :]
    b0 = bsx[0:1, :]
    e64 = _head_select(SSD_HEAD_DIM)

    x2 = x_prompt.reshape(N_TOK, D_MODEL)
    hp1, gv_p, ssm_p, conv8 = _p_mixer(x2, g_mix, wi, wdt_in, lng, lnb, ws, bsx, cw, cb, dtb8, a8,
                                       dsk, sn, _triu_ones(), _chunk_select(), wo)
    mk, mv, mkb, mvb = _memkv(mem_prompt.reshape(BATCH * N_MEM, D_MODEL), g_mem, wk, wv)
    hp2 = _p_attn(hp1, g_x, wq, mkb.reshape(BATCH, N_MEM, D_MODEL),
                  mvb.reshape(BATCH, N_MEM, D_MODEL), wxo)
    y_prompt = _p_ffn(hp2, g_ffn, wg, wu, wd, g_fin).reshape(BATCH, SEQ, D_MODEL)

    xs2 = x_sample.reshape(DEC_BATCH, D_MODEL)
    sc = state_conv[l].reshape(DEC_BATCH, (CONV_W - 1) * CONV_DIM)
    gout, v_s, z_s, conv_s, xs_s, xdt_s, bc_s, dec_s = _s_inproj(
        xs2, g_mix, wi, wdt_in, lng, lnb, sc, cw, cb, dtb, a_neg, w00, b0, e64)
    h0 = state_ssm[l].reshape(DEC_BATCH, D_SSD, D_STATE)
    ssm_s, y_s = _s_state(dec_s[:, :SSD_HEADS].reshape(-1), h0, xdt_s, bc_s)
    hs, q_s = _s_mixout(xs2, gout, y_s, xs_s, z_s, dsk, sn, wo, g_x, wq)
    o_s = _s_attn(q_s.reshape(DEC_BATCH, X_HEADS, X_HD), cache_mem_k[l], cache_mem_v[l])
    o_s = o_s.reshape(DEC_BATCH, D_MODEL)
    y_sample = _s_ffn(hs, o_s, wxo, g_ffn, wg, wu, wd, g_fin).reshape(DEC_BATCH, 1, D_MODEL)

    return (
        y_prompt,
        y_sample,
        ssm_p.reshape(1, BATCH, SSD_HEADS, SSD_HEAD_DIM, D_STATE),
        conv8[:, SUBLANES - (CONV_W - 1):, :][None],
        gv_p[None],
        mk.reshape(1, BATCH, N_MEM, X_HEADS, X_HD),
        mv.reshape(1, BATCH, N_MEM, X_HEADS, X_HD),
        ssm_s.reshape(1, DEC_BATCH, SSD_HEADS, SSD_HEAD_DIM, D_STATE),
        conv_s.reshape(1, DEC_BATCH, CONV_W - 1, CONV_DIM),
        v_s.reshape(1, DEC_BATCH, 1, D_GMLP),
    )
```

```python
import numpy as np
import jax
import jax.numpy as jnp
from jax import lax
from jax.experimental import pallas as pl
from jax.experimental.pallas import tpu as pltpu

f32 = jnp.float32
bf16 = jnp.bfloat16

D_MODEL = 1024
BATCH = 8
SEQ = 2048
DEC_BATCH = 128
D_GMLP = 512
GMLP_HEADS = 4
GMLP_HD = 128
CHUNK = 128
D_SSD = 512
SSD_HEAD_DIM = 64
SSD_HEADS = 8
SSD_GROUPS = 2
D_STATE = 128
CONV_W = 4
CONV_DIM = 1024
N_MEM = 256
X_HEADS = 4
X_HD = 256
D_FF = 2816
EPS = 1e-6

N_TOK = BATCH * SEQ
TM = 512
LANES = 128
SUBLANES = 8
VMEM_LIMIT = 56 * 1024 * 1024
SB = 8
NCH = 4
TMIX = NCH * CHUNK


def _cparams(sem):
    return pltpu.CompilerParams(dimension_semantics=sem, vmem_limit_bytes=VMEM_LIMIT)


def _const_spec(shape):
    nd = len(shape)
    return pl.BlockSpec(shape, lambda *_: (0,) * nd, pipeline_mode=pl.Buffered(1))


def _whole_spec(shape):
    nd = len(shape)
    return pl.BlockSpec(shape, lambda *_: (0,) * nd)


def _dot(a, b):
    return jnp.dot(a, b, preferred_element_type=f32)


def _dot_nt(a, b):
    return lax.dot_general(a, b, (((1,), (1,)), ((), ())), preferred_element_type=f32)


def _dot_tn(a, b):
    return lax.dot_general(a, b, (((0,), (0,)), ((), ())), preferred_element_type=f32)


def _rmsnorm(x, g):
    return x * lax.rsqrt(jnp.mean(x * x, axis=-1, keepdims=True) + EPS) * g


def _layernorm(x, g, b):
    mu = jnp.mean(x, axis=-1, keepdims=True)
    xc = x - mu
    return xc * lax.rsqrt(jnp.mean(xc * xc, axis=-1, keepdims=True) + EPS) * g + b


LOG2E = 1.4426950408889634
GELU_C = 0.7978845608028654
GELU_A = 0.044715


def _sigmoid(x):
    return 1.0 / (1.0 + jnp.exp2(x * (-LOG2E)))


def _silu(x):
    return x * _sigmoid(x)


def _gelu_tanh(x):
    k1 = -2.0 * GELU_C * LOG2E
    k2 = k1 * GELU_A
    return x / (1.0 + jnp.exp2(x * (k1 + k2 * (x * x))))


def _softplus(x):
    return jnp.maximum(x, 0.0) + jnp.log1p(jnp.exp(-jnp.abs(x)))


def _exact_parts(x, terms):
    parts = []
    r = x
    for _ in range(terms):
        p = r.astype(bf16).astype(f32)
        parts.append(p)
        r = r - p
    return parts


def _select_dot(x, sel, terms):
    m = x.shape[0]
    stacked = jnp.concatenate(_exact_parts(x, terms), axis=0).astype(bf16)
    r = _dot(stacked, sel)
    out = r[0:m]
    for i in range(1, terms):
        out = out + r[i * m:(i + 1) * m]
    return out


def _gated_group_norm(y, z, sn):
    yz = y * _silu(z)
    half = D_SSD // SSD_GROUPS
    outs = []
    for g in range(SSD_GROUPS):
        blk = yz[:, g * half:(g + 1) * half]
        outs.append(blk * lax.rsqrt(jnp.mean(blk * blk, axis=-1, keepdims=True) + EPS))
    return jnp.concatenate(outs, axis=1) * sn


def _in_proj(xn, wi_ref, wdt, lng, lnb):
    o1, o2, o3 = 2 * D_GMLP, 2 * D_GMLP + D_SSD, 2 * D_GMLP + D_SSD + CONV_DIM
    gu = _gelu_tanh(_dot(xn, wi_ref[:, 0:D_GMLP]))
    v = _layernorm(_gelu_tanh(_dot(xn, wi_ref[:, D_GMLP:o1])), lng, lnb)
    z = _dot(xn, wi_ref[:, o1:o2])
    xbc = _dot(xn, wi_ref[:, o2:o3])
    dtr = _dot(xn, wdt)
    return gu, v, z, xbc, dtr


def _p_mixer_body(x_ref, g_ref, wi_ref, wdt_ref, lng_ref, lnb_ref,
                  ws_ref, bsx_ref, cw_ref, cb_ref, dtb8_ref, a8_ref, dsk_ref, sn_ref,
                  triu_ref, esel_ref, wo_ref,
                  hp_ref, gv_ref, ssm_ref, conv_ref,
                  xp_scr, h_scr):
    j = pl.program_id(1)

    @pl.when(j == 0)
    def _():
        xp_scr[0:SUBLANES, :] = jnp.zeros((SUBLANES, CONV_DIM), f32)
        h_scr[...] = jnp.zeros_like(h_scr)

    t_idx = lax.broadcasted_iota(jnp.int32, (CHUNK, CHUNK), 0)
    s_idx = lax.broadcasted_iota(jnp.int32, (CHUNK, CHUNK), 1)
    causal = t_idx >= s_idx
    w_tril = [jnp.where(causal, ws_ref[hh], 0.0).astype(bf16) for hh in range(GMLP_HEADS)]
    gw = D_SSD // SSD_GROUPS
    hpg = SSD_HEADS // SSD_GROUPS
    zeros_pad = jnp.zeros((CHUNK - 3 * SSD_HEADS, CHUNK), f32)

    xn_all = _rmsnorm(x_ref[...], g_ref[...]).astype(bf16)
    o1, o2, o3 = 2 * D_GMLP, 2 * D_GMLP + D_SSD, 2 * D_GMLP + D_SSD + CONV_DIM

    def xn_rows(ci):
        return xn_all[ci * CHUNK:(ci + 1) * CHUNK, :]

    def out_project(ci, mix):
        rows = slice(ci * CHUNK, (ci + 1) * CHUNK)
        hp_ref[rows, :] = x_ref[rows, :] + _dot(mix, wo_ref[...])

    cur = _in_proj(xn_rows(0), wi_ref, wdt_ref[...], lng_ref[...], lnb_ref[...])
    prev_mix = None
    for ci in range(NCH):
        gu, v, z, xbc, dtr = cur
        nxt = ci + 1 < NCH

        xp_scr[SUBLANES:SUBLANES + CHUNK, :] = xbc
        acc = cb_ref[...]
        for k in range(CONV_W - 1):
            acc = acc + xp_scr[pl.ds(SUBLANES - (CONV_W - 1) + k, CHUNK), :] * cw_ref[k:k + 1, :]
        acc = acc + xbc * cw_ref[CONV_W - 1:CONV_W, :]
        xc = _silu(acc)
        tail = xbc[CHUNK - SUBLANES:, :]
        xp_scr[0:SUBLANES, :] = tail

        xs = xc[:, :D_SSD]
        b_all = xc[:, D_SSD:D_SSD + SSD_GROUPS * D_STATE].astype(bf16)
        c_all = xc[:, D_SSD + SSD_GROUPS * D_STATE:].astype(bf16)

        dt8 = _softplus(dtr.T[0:SSD_HEADS, :] + dtb8_ref[...])
        a8 = dt8 * a8_ref[...]
        a_parts = jnp.concatenate(_exact_parts(a8, 3) + [zeros_pad], axis=0).astype(bf16)
        a_sum = _dot(a_parts, triu_ref[...])
        if nxt:
            n_gu = _gelu_tanh(_dot(xn_rows(ci + 1), wi_ref[:, 0:D_GMLP]))
        acum8 = a_sum[0:8] + a_sum[8:16] + a_sum[16:24]
        a_last8 = acum8[:, CHUNK - 1:CHUNK]
        expa8 = jnp.exp(acum8)
        wgt8 = jnp.exp(a_last8 - acum8) * dt8
        cd8 = jnp.exp(a_last8)

        stacked = jnp.concatenate([dt8, expa8, wgt8, acum8], axis=0)
        lhs = jnp.concatenate(_exact_parts(stacked, 3) + [jnp.zeros((32, CHUNK), f32)], axis=0)
        bc = _dot(lhs.T.astype(bf16), esel_ref[...])
        dt_even = bc[:, 0:D_SSD]
        dt_odd = bc[:, D_SSD:2 * D_SSD]
        expa_e = bc[:, 2 * D_SSD:3 * D_SSD]
        wgt_e = bc[:, 3 * D_SSD:4 * D_SSD]
        col_b = bc[:, 4 * D_SSD:]

        cb = []
        for g in range(SSD_GROUPS):
            sl = slice(g * D_STATE, (g + 1) * D_STATE)
            cb.append(_dot_nt(c_all[:, sl], b_all[:, sl]))
        if nxt:
            n_v = _layernorm(_gelu_tanh(_dot(xn_rows(ci + 1), wi_ref[:, D_GMLP:o1])),
                             lng_ref[...], lnb_ref[...])

        vb = v.astype(bf16)
        mixed = jnp.concatenate(
            [_dot(w_tril[hh], vb[:, hh * GMLP_HD:(hh + 1) * GMLP_HD]) for hh in range(GMLP_HEADS)],
            axis=1) + bsx_ref[...]
        g_out = gu * mixed

        h_prev = h_scr[...].astype(bf16)
        y_off = jnp.concatenate(
            [_dot_nt(c_all[:, g * D_STATE:(g + 1) * D_STATE], h_prev[g * gw:(g + 1) * gw, :])
             for g in range(SSD_GROUPS)], axis=1) * expa_e
        if nxt:
            n_z = _dot(xn_rows(ci + 1), wi_ref[:, o1:o2])

        xw = (xs * wgt_e).astype(bf16)
        states = [_dot_tn(xw[:, g * gw:(g + 1) * gw], b_all[:, g * D_STATE:(g + 1) * D_STATE])
                  for g in range(SSD_GROUPS)]
        for h in range(SSD_HEADS):
            g, jh = h // hpg, h % hpg
            hr = slice(h * SSD_HEAD_DIM, (h + 1) * SSD_HEAD_DIM)
            h_scr[hr, :] = (h_scr[hr, :] * cd8[h:h + 1, :]
                            + states[g][jh * SSD_HEAD_DIM:(jh + 1) * SSD_HEAD_DIM, :])
        if nxt:
            n_xbc = _dot(xn_rows(ci + 1), wi_ref[:, o2:o3])
            n_dtr = _dot(xn_rows(ci + 1), wdt_ref[...])

        xdt_lo = xs * dt_even
        xdt_hi = xs * dt_odd
        y_diag = []
        for q in range(SSD_HEADS // 2):
            g = (2 * q) // hpg
            w_pair = []
            for h in (2 * q, 2 * q + 1):
                seg = col_b[:, h * CHUNK:(h + 1) * CHUNK] - acum8[h:h + 1, :]
                decay = jnp.exp(jnp.where(causal, seg, -jnp.inf))
                w_pair.append((cb[g] * decay).astype(bf16))
            sl = slice(q * LANES, (q + 1) * LANES)
            rhs = jnp.concatenate([xdt_lo[:, sl], xdt_hi[:, sl]], axis=0).astype(bf16)
            y_diag.append(_dot(jnp.concatenate(w_pair, axis=1), rhs))
        y_diag = jnp.concatenate(y_diag, axis=1)
        if prev_mix is not None:
            out_project(ci - 1, prev_mix)

        y = y_diag + y_off
        y = y + dsk_ref[...] * xs
        s_out = _gated_group_norm(y, z, sn_ref[...])
        prev_mix = jnp.concatenate([g_out, s_out], axis=1).astype(bf16)
        if nxt:
            cur = (n_gu, n_v, n_z, n_xbc, n_dtr)
    out_project(NCH - 1, prev_mix)

    gv_ref[0] = v
    conv_ref[0] = tail

    @pl.when(j == SEQ // TMIX - 1)
    def _():
        ssm_ref[0] = h_scr[...]


def _p_mixer(x2, g, wi, wdt, lng, lnb, ws, bsx, cw, cb, dtb8, a8, dsk, sn, triu, esel, wo):
    tiles = SEQ // TMIX
    row = lambda b, j: (b * tiles + j, 0)
    per_b = lambda b, j: (b, 0, 0)
    consts = (g, wi, wdt, lng, lnb, ws, bsx, cw, cb, dtb8, a8, dsk, sn, triu, esel, wo)
    return pl.pallas_call(
        _p_mixer_body,
        grid=(BATCH, tiles),
        in_specs=[pl.BlockSpec((TMIX, D_MODEL), row)] + [_const_spec(a.shape) for a in consts],
        out_specs=[
            pl.BlockSpec((TMIX, D_MODEL), row),
            pl.BlockSpec((1, CHUNK, D_GMLP), per_b),
            pl.BlockSpec((1, D_SSD, D_STATE), per_b),
            pl.BlockSpec((1, SUBLANES, CONV_DIM), per_b),
        ],
        out_shape=[
            jax.ShapeDtypeStruct((N_TOK, D_MODEL), f32),
            jax.ShapeDtypeStruct((BATCH, CHUNK, D_GMLP), f32),
            jax.ShapeDtypeStruct((BATCH, D_SSD, D_STATE), f32),
            jax.ShapeDtypeStruct((BATCH, SUBLANES, CONV_DIM), f32),
        ],
        scratch_shapes=[
            pltpu.VMEM((SUBLANES + CHUNK, CONV_DIM), f32),
            pltpu.VMEM((D_SSD, D_STATE), f32),
        ],
        compiler_params=_cparams(("arbitrary", "arbitrary")),
        name="p_mixer",
    )(x2, *consts)


def _memkv_body(m_ref, g_ref, wk_ref, wv_ref, k4_ref, v4_ref, kb_ref, vb_ref):
    mn = _rmsnorm(m_ref[...], g_ref[...]).astype(bf16)
    for w_ref, o4_ref, ob_ref in ((wk_ref, k4_ref, kb_ref), (wv_ref, v4_ref, vb_ref)):
        r = _dot(mn, w_ref[...])
        ob_ref[...] = r.astype(bf16)
        for h in range(X_HEADS):
            o4_ref[:, h, :] = r[:, h * X_HD:(h + 1) * X_HD]


def _memkv(mem2, g, wk, wv):
    n = mem2.shape[0]
    row = lambda i: (i, 0)
    row4 = lambda i: (i, 0, 0)
    spec4 = pl.BlockSpec((TM, X_HEADS, X_HD), row4)
    spec2 = pl.BlockSpec((TM, D_MODEL), row)
    shape4 = jax.ShapeDtypeStruct((n, X_HEADS, X_HD), f32)
    shape2 = jax.ShapeDtypeStruct((n, D_MODEL), bf16)
    return pl.pallas_call(
        _memkv_body,
        grid=(n // TM,),
        in_specs=[
            spec2,
            _const_spec((1, D_MODEL)),
            _const_spec((D_MODEL, D_MODEL)),
            _const_spec((D_MODEL, D_MODEL)),
        ],
        out_specs=[spec4, spec4, spec2, spec2],
        out_shape=[shape4, shape4, shape2, shape2],
        compiler_params=_cparams(("arbitrary",)),
        name="memkv",
    )(mem2, g, wk, wv)


def _p_attn_body(hp_ref, gx_ref, wq_ref, k_ref, v_ref, wxo_ref, out_ref):
    hp = hp_ref[...]
    hn = _rmsnorm(hp, gx_ref[...]).astype(bf16)
    q = _dot(hn, wq_ref[...]).astype(bf16)
    kb = k_ref[0]
    vb = v_ref[0]
    outs = []
    for h in range(X_HEADS):
        sl = slice(h * X_HD, (h + 1) * X_HD)
        s = _dot_nt(q[:, sl], kb[:, sl]) * (X_HD ** -0.5)
        e = jnp.exp(s - jnp.max(s, axis=-1, keepdims=True))
        p = e / jnp.sum(e, axis=-1, keepdims=True)
        outs.append(_dot(p.astype(bf16), vb[:, sl]))
    o = jnp.concatenate(outs, axis=1).astype(bf16)
    out_ref[...] = hp + _dot(o, wxo_ref[...])


def _p_attn(hp, gx, wq, mk, mv, wxo):
    tiles = SEQ // TM
    row = lambda b, j: (b * tiles + j, 0)
    per_b = lambda b, j: (b, 0, 0)
    return pl.pallas_call(
        _p_attn_body,
        grid=(BATCH, tiles),
        in_specs=[
            pl.BlockSpec((TM, D_MODEL), row),
            _const_spec((1, D_MODEL)),
            _const_spec((D_MODEL, D_MODEL)),
            pl.BlockSpec((1, N_MEM, D_MODEL), per_b),
            pl.BlockSpec((1, N_MEM, D_MODEL), per_b),
            _const_spec((D_MODEL, D_MODEL)),
        ],
        out_specs=pl.BlockSpec((TM, D_MODEL), row),
        out_shape=jax.ShapeDtypeStruct((N_TOK, D_MODEL), f32),
        compiler_params=_cparams(("arbitrary", "arbitrary")),
        name="p_attn",
    )(hp, gx, wq, mk, mv, wxo)


def _ffn(hp, gf, wg, wu, wd, gfin):
    hn = _rmsnorm(hp, gf).astype(bf16)
    hid = (_silu(_dot(hn, wg)) * _dot(hn, wu)).astype(bf16)
    return _rmsnorm(hp + _dot(hid, wd), gfin)


def _ffn_body(h_ref, gf_ref, wg_ref, wu_ref, wd_ref, gfin_ref, y_ref):
    y_ref[...] = _ffn(h_ref[...], gf_ref[...], wg_ref[...], wu_ref[...], wd_ref[...], gfin_ref[...])


def _p_ffn(hp, gf, wg, wu, wd, gfin):
    row = lambda i: (i, 0)
    return pl.pallas_call(
        _ffn_body,
        grid=(N_TOK // TM,),
        in_specs=[
            pl.BlockSpec((TM, D_MODEL), row),
            _const_spec((1, D_MODEL)),
            _const_spec((D_MODEL, D_FF)),
            _const_spec((D_MODEL, D_FF)),
            _const_spec((D_FF, D_MODEL)),
            _const_spec((1, D_MODEL)),
        ],
        out_specs=pl.BlockSpec((TM, D_MODEL), row),
        out_shape=jax.ShapeDtypeStruct((N_TOK, D_MODEL), f32),
        compiler_params=_cparams(("arbitrary",)),
        name="p_ffn",
    )(hp, gf, wg, wu, wd, gfin)


def _s_inproj_body(x_ref, g_ref, wi_ref, wdt_ref, lng_ref, lnb_ref,
                   sc_ref, cw_ref, cb_ref, dtb_ref, a_ref, w00_ref, b0_ref, e64_ref,
                   gout_ref, v_ref, z_ref, conv_ref, xs_ref, xdt_ref, bc_ref, dec_ref):
    xn = _rmsnorm(x_ref[...], g_ref[...]).astype(bf16)
    gu, v, z, xbc, dtr = _in_proj(xn, wi_ref, wdt_ref[...], lng_ref[...], lnb_ref[...])
    v_ref[...] = v
    z_ref[...] = z
    gout_ref[...] = gu * (v * w00_ref[...] + b0_ref[...])

    acc = cb_ref[...]
    for k in range(CONV_W - 1):
        acc = acc + sc_ref[:, k * CONV_DIM:(k + 1) * CONV_DIM] * cw_ref[k:k + 1, :]
    acc = acc + xbc * cw_ref[CONV_W - 1:CONV_W, :]
    xc = _silu(acc)
    conv_ref[:, 0:(CONV_W - 2) * CONV_DIM] = sc_ref[:, CONV_DIM:(CONV_W - 1) * CONV_DIM]
    conv_ref[:, (CONV_W - 2) * CONV_DIM:] = xbc

    xs = xc[:, :D_SSD]
    dt = _softplus(dtr + dtb_ref[...])
    dec_ref[...] = jnp.exp(dt * a_ref[...])
    xs_ref[...] = xs
    xdt_ref[...] = xs * _select_dot(dt, e64_ref[...], 2)
    bc_ref[...] = xc[:, D_SSD:]


def _s_inproj(x, g, wi, wdt, lng, lnb, sc, cw, cb, dtb, a_neg, w00, b0, e64):
    nb = DEC_BATCH
    args = (x, g, wi, wdt, lng, lnb, sc, cw, cb, dtb, a_neg, w00, b0, e64)
    return pl.pallas_call(
        _s_inproj_body,
        grid=(1,),
        in_specs=[_const_spec(a.shape) for a in args],
        out_specs=[
            _whole_spec((nb, D_GMLP)), _whole_spec((nb, D_GMLP)), _whole_spec((nb, D_SSD)),
            _whole_spec((nb, (CONV_W - 1) * CONV_DIM)), _whole_spec((nb, D_SSD)),
            _whole_spec((nb, D_SSD)), _whole_spec((nb, 2 * SSD_GROUPS * D_STATE)),
            _whole_spec((nb, LANES)),
        ],
        out_shape=[
            jax.ShapeDtypeStruct((nb, D_GMLP), f32), jax.ShapeDtypeStruct((nb, D_GMLP), f32),
            jax.ShapeDtypeStruct((nb, D_SSD), f32),
            jax.ShapeDtypeStruct((nb, (CONV_W - 1) * CONV_DIM), f32),
            jax.ShapeDtypeStruct((nb, D_SSD), f32), jax.ShapeDtypeStruct((nb, D_SSD), f32),
            jax.ShapeDtypeStruct((nb, 2 * SSD_GROUPS * D_STATE), f32),
            jax.ShapeDtypeStruct((nb, LANES), f32),
        ],
        compiler_params=_cparams(("arbitrary",)),
        name="s_inproj",
    )(*args)


def _s_state_body(dec_ref, h0_ref, xdt_ref, bc_ref, h_ref, y_ref):
    step = pl.program_id(0)
    gw = D_SSD // SSD_GROUPS
    hpg = SSD_HEADS // SSD_GROUPS
    nbc = SSD_GROUPS * D_STATE
    xdt = xdt_ref[...]
    bc = bc_ref[...]
    row = lax.broadcasted_iota(jnp.int32, (SB, D_STATE), 0)
    y = [jnp.zeros((SB, gw), f32) for _ in range(SSD_GROUPS)]
    for i in range(SB):
        only_i = row == i
        for g in range(SSD_GROUPS):
            b_i = jnp.where(only_i, bc[:, g * D_STATE:(g + 1) * D_STATE], 0.0)
            c_i = jnp.where(only_i, bc[:, nbc + g * D_STATE:nbc + (g + 1) * D_STATE], 0.0)
            upd = _dot_tn(xdt[:, g * gw:(g + 1) * gw], b_i)
            blocks = []
            for j in range(hpg):
                h = g * hpg + j
                rows = slice(h * SSD_HEAD_DIM, (h + 1) * SSD_HEAD_DIM)
                d = dec_ref[(step * SB + i) * SSD_HEADS + h]
                blocks.append(h0_ref[i, rows, :] * d + upd[j * SSD_HEAD_DIM:(j + 1) * SSD_HEAD_DIM, :])
            h_new = jnp.concatenate(blocks, axis=0)
            h_ref[i, g * gw:(g + 1) * gw, :] = h_new
            y[g] = y[g] + _dot_nt(c_i.astype(bf16), h_new.astype(bf16))
    y_ref[...] = jnp.concatenate(y, axis=1)


def _s_state(dec, h0, xdt, bc):
    blk = lambda i: (i, 0, 0)
    row = lambda i: (i, 0)
    return pl.pallas_call(
        _s_state_body,
        grid=(DEC_BATCH // SB,),
        in_specs=[
            pl.BlockSpec(memory_space=pltpu.SMEM),
            pl.BlockSpec((SB, D_SSD, D_STATE), blk),
            pl.BlockSpec((SB, D_SSD), row),
            pl.BlockSpec((SB, 2 * SSD_GROUPS * D_STATE), row),
        ],
        out_specs=[pl.BlockSpec((SB, D_SSD, D_STATE), blk), pl.BlockSpec((SB, D_SSD), row)],
        out_shape=[jax.ShapeDtypeStruct((DEC_BATCH, D_SSD, D_STATE), f32),
                   jax.ShapeDtypeStruct((DEC_BATCH, D_SSD), f32)],
        compiler_params=_cparams(("arbitrary",)),
        name="s_state",
    )(dec, h0, xdt, bc)


def _s_mixout_body(x_ref, gout_ref, y_ref, xs_ref, z_ref, dsk_ref, sn_ref, wo_ref, gx_ref, wq_ref,
                   hs_ref, q_ref):
    y = y_ref[...] + dsk_ref[...] * xs_ref[...]
    s_out = _gated_group_norm(y, z_ref[...], sn_ref[...])
    mix = jnp.concatenate([gout_ref[...], s_out], axis=1).astype(bf16)
    hs = x_ref[...] + _dot(mix, wo_ref[...])
    hs_ref[...] = hs
    q_ref[...] = _dot(_rmsnorm(hs, gx_ref[...]).astype(bf16), wq_ref[...])


def _s_mixout(x, gout, y, xs, z, dsk, sn, wo, gx, wq):
    args = (x, gout, y, xs, z, dsk, sn, wo, gx, wq)
    return pl.pallas_call(
        _s_mixout_body,
        grid=(1,),
        in_specs=[_const_spec(a.shape) for a in args],
        out_specs=[_whole_spec((DEC_BATCH, D_MODEL)), _whole_spec((DEC_BATCH, D_MODEL))],
        out_shape=[jax.ShapeDtypeStruct((DEC_BATCH, D_MODEL), f32),
                   jax.ShapeDtypeStruct((DEC_BATCH, D_MODEL), f32)],
        compiler_params=_cparams(("arbitrary",)),
        name="s_mixout",
    )(*args)


def _s_attn_copies(k_hbm, v_hbm, kbuf, vbuf, sem, step, slot):
    rows = pl.ds(step * SB, SB)
    out = []
    for h in range(X_HEADS):
        out.append(pltpu.make_async_copy(k_hbm.at[rows, :, h, :], kbuf.at[slot, h], sem.at[slot, 0, h]))
        out.append(pltpu.make_async_copy(v_hbm.at[rows, :, h, :], vbuf.at[slot, h], sem.at[slot, 1, h]))
    return out


def _s_attn_body(q_ref, k_hbm, v_hbm, o_ref, kbuf, vbuf, sem):
    step = pl.program_id(0)
    slot = step % 2
    copies = lambda s, sl: _s_attn_copies(k_hbm, v_hbm, kbuf, vbuf, sem, s, sl)

    @pl.when(step == 0)
    def _():
        for c in copies(0, 0):
            c.start()

    @pl.when(step + 1 < pl.num_programs(0))
    def _():
        for c in copies(step + 1, 1 - slot):
            c.start()

    for c in copies(step, slot):
        c.wait()

    def one(i, carry):
        for h in range(X_HEADS):
            sl = slice(h * X_HD, (h + 1) * X_HD)
            q = q_ref[pl.ds(i, 1), sl]
            s = jnp.sum(kbuf[slot, h, i] * q, axis=-1, keepdims=True) * (X_HD ** -0.5)
            e = jnp.exp(s - jnp.max(s, axis=0, keepdims=True))
            p = e / jnp.sum(e, axis=0, keepdims=True)
            o_ref[pl.ds(i, 1), sl] = jnp.sum(p * vbuf[slot, h, i], axis=0, keepdims=True)
        return carry

    lax.fori_loop(0, SB, one, 0)


def _s_attn(q, ck, cv):
    row = lambda i: (i, 0)
    return pl.pallas_call(
        _s_attn_body,
        grid=(DEC_BATCH // SB,),
        in_specs=[
            pl.BlockSpec((SB, D_MODEL), row),
            pl.BlockSpec(memory_space=pl.ANY),
            pl.BlockSpec(memory_space=pl.ANY),
        ],
        out_specs=pl.BlockSpec((SB, D_MODEL), row),
        out_shape=jax.ShapeDtypeStruct((DEC_BATCH, D_MODEL), f32),
        scratch_shapes=[
            pltpu.VMEM((2, X_HEADS, SB, N_MEM, X_HD), f32),
            pltpu.VMEM((2, X_HEADS, SB, N_MEM, X_HD), f32),
            pltpu.SemaphoreType.DMA((2, 2, X_HEADS)),
        ],
        compiler_params=_cparams(("arbitrary",)),
        name="s_attn",
    )(q, ck, cv)


def _s_ffn_body(hs_ref, o_ref, wxo_ref, gf_ref, wg_ref, wu_ref, wd_ref, gfin_ref, y_ref):
    hs = hs_ref[...] + _dot(o_ref[...].astype(bf16), wxo_ref[...])
    y_ref[...] = _ffn(hs, gf_ref[...], wg_ref[...], wu_ref[...], wd_ref[...], gfin_ref[...])


def _s_ffn(hs, o, wxo, gf, wg, wu, wd, gfin):
    args = (hs, o, wxo, gf, wg, wu, wd, gfin)
    return pl.pallas_call(
        _s_ffn_body,
        grid=(1,),
        in_specs=[_const_spec(a.shape) for a in args],
        out_specs=_whole_spec((DEC_BATCH, D_MODEL)),
        out_shape=jax.ShapeDtypeStruct((DEC_BATCH, D_MODEL), f32),
        compiler_params=_cparams(("arbitrary",)),
        name="s_ffn",
    )(*args)


def _triu_ones():
    return jnp.asarray(np.triu(np.ones((CHUNK, CHUNK), np.float32)), bf16)


def _chunk_select():
    e = np.zeros((LANES, 4 * D_SSD + SSD_HEADS * CHUNK), np.float32)
    for part in range(3):
        for h in range(SSD_HEADS):
            lanes = slice(h * SSD_HEAD_DIM, (h + 1) * SSD_HEAD_DIM)
            r = part * 32 + h
            e[r, (h % 2) * D_SSD:(h % 2 + 1) * D_SSD][lanes] = 1.0
            e[r + SSD_HEADS, 2 * D_SSD:3 * D_SSD][lanes] = 1.0
            e[r + 2 * SSD_HEADS, 3 * D_SSD:4 * D_SSD][lanes] = 1.0
            e[r + 3 * SSD_HEADS, 4 * D_SSD + h * CHUNK:4 * D_SSD + (h + 1) * CHUNK] = 1.0
    return jnp.asarray(e, bf16)


def _head_select(width):
    e = np.zeros((LANES, SSD_HEADS * width), np.float32)
    for h in range(SSD_HEADS):
        e[h, h * width:(h + 1) * width] = 1.0
    return jnp.asarray(e, bf16)


def _pad_lanes(v):
    return jnp.pad(v.astype(f32), (0, LANES - v.shape[0]))[None, :]


def kernel(x_prompt, x_sample, state_ssm, state_conv, cache_mem_k, cache_mem_v, mem_prompt,
           norm_mix, w_in, gmlp_ln_g, gmlp_ln_b, gmlp_ws, gmlp_bs, conv_w, conv_b, dt_bias,
           a_log, d_skip, ssd_norm, w_out, norm_xattn, norm_mem, w_q, w_k, w_v, w_xo,
           norm_ffn, w_gate, w_up, w_down, norm_final):
    l = 0
    wi = w_in[l].astype(bf16)
    o3 = 2 * D_GMLP + D_SSD + CONV_DIM
    wdt_in = jnp.pad(wi[:, o3:], ((0, 0), (0, LANES - SSD_HEADS)))
    wo = w_out[l].astype(bf16)
    wq, wk, wv, wxo = (w[l].astype(bf16) for w in (w_q, w_k, w_v, w_xo))
    wg, wu, wd = (w[l].astype(bf16) for w in (w_gate, w_up, w_down))

    g_mix, g_x, g_mem, g_ffn = (g[l][None, :] for g in (norm_mix, norm_xattn, norm_mem, norm_ffn))
    g_fin = norm_final[None, :]
    lng, lnb = gmlp_ln_g[l][None, :], gmlp_ln_b[l][None, :]
    cw, cb = conv_w[l], conv_b[l][None, :]
    a_vec = -jnp.exp(a_log[l].astype(f32))
    dtb = _pad_lanes(dt_bias[l])
    a_neg = _pad_lanes(a_vec)
    dtb8 = jnp.broadcast_to(dt_bias[l].astype(f32)[:, None], (SSD_HEADS, CHUNK))
    a8 = jnp.broadcast_to(a_vec[:, None], (SSD_HEADS, CHUNK))
    dsk = jnp.repeat(d_skip[l].astype(f32), SSD_HEAD_DIM)[None, :]
    sn = ssd_norm[l][None, :]
    ws = gmlp_ws[l]
    bsx = jnp.repeat(jnp.swapaxes(gmlp_bs[l], 0, 1), GMLP_HD, axis=1)
    w00 = jnp.repeat(ws[:, 0, 0], GMLP_HD)[None, :]
    b0 = bsx[0:1, :]
    e64 = _head_select(SSD_HEAD_DIM)

    x2 = x_prompt.reshape(N_TOK, D_MODEL)
    hp1, gv_p, ssm_p, conv8 = _p_mixer(x2, g_mix, wi, wdt_in, lng, lnb, ws, bsx, cw, cb, dtb8, a8,
                                       dsk, sn, _triu_ones(), _chunk_select(), wo)
    mk, mv, mkb, mvb = _memkv(mem_prompt.reshape(BATCH * N_MEM, D_MODEL), g_mem, wk, wv)
    hp2 = _p_attn(hp1, g_x, wq, mkb.reshape(BATCH, N_MEM, D_MODEL),
                  mvb.reshape(BATCH, N_MEM, D_MODEL), wxo)
    y_prompt = _p_ffn(hp2, g_ffn, wg, wu, wd, g_fin).reshape(BATCH, SEQ, D_MODEL)

    xs2 = x_sample.reshape(DEC_BATCH, D_MODEL)
    sc = state_conv[l].reshape(DEC_BATCH, (CONV_W - 1) * CONV_DIM)
    gout, v_s, z_s, conv_s, xs_s, xdt_s, bc_s, dec_s = _s_inproj(
        xs2, g_mix, wi, wdt_in, lng, lnb, sc, cw, cb, dtb, a_neg, w00, b0, e64)
    h0 = state_ssm[l].reshape(DEC_BATCH, D_SSD, D_STATE)
    ssm_s, y_s = _s_state(dec_s[:, :SSD_HEADS].reshape(-1), h0, xdt_s, bc_s)
    hs, q_s = _s_mixout(xs2, gout, y_s, xs_s, z_s, dsk, sn, wo, g_x, wq)
    o_s = _s_attn(q_s, cache_mem_k[l], cache_mem_v[l])
    y_sample = _s_ffn(hs, o_s, wxo, g_ffn, wg, wu, wd, g_fin).reshape(DEC_BATCH, 1, D_MODEL)

    return (
        y_prompt,
        y_sample,
        ssm_p.reshape(1, BATCH, SSD_HEADS, SSD_HEAD_DIM, D_STATE),
        conv8[:, SUBLANES - (CONV_W - 1):, :][None],
        gv_p[None],
        mk.reshape(1, BATCH, N_MEM, X_HEADS, X_HD),
        mv.reshape(1, BATCH, N_MEM, X_HEADS, X_HD),
        ssm_s.reshape(1, DEC_BATCH, SSD_HEADS, SSD_HEAD_DIM, D_STATE),
        conv_s.reshape(1, DEC_BATCH, CONV_W - 1, CONV_DIM),
        v_s.reshape(1, DEC_BATCH, 1, D_GMLP),
    )
```

```python
import numpy as np
import jax
import jax.numpy as jnp
from jax import lax
from jax.experimental import pallas as pl
from jax.experimental.pallas import tpu as pltpu

f32 = jnp.float32
bf16 = jnp.bfloat16

D_MODEL = 1024
BATCH = 8
SEQ = 2048
DEC_BATCH = 128
D_GMLP = 512
GMLP_HEADS = 4
GMLP_HD = 128
CHUNK = 128
D_SSD = 512
SSD_HEAD_DIM = 64
SSD_HEADS = 8
SSD_GROUPS = 2
D_STATE = 128
CONV_W = 4
CONV_DIM = 1024
N_MEM = 256
X_HEADS = 4
X_HD = 256
D_FF = 2816
EPS = 1e-6

N_TOK = BATCH * SEQ
TM = 512
LANES = 128
SUBLANES = 8
VMEM_LIMIT = 56 * 1024 * 1024
SB = 8
SA = DEC_BATCH // (N_TOK // TM)
NCH = 4
TMIX = NCH * CHUNK


def _cparams(sem):
    return pltpu.CompilerParams(dimension_semantics=sem, vmem_limit_bytes=VMEM_LIMIT)


def _const_spec(shape):
    nd = len(shape)
    return pl.BlockSpec(shape, lambda *_: (0,) * nd, pipeline_mode=pl.Buffered(1))


def _whole_spec(shape):
    nd = len(shape)
    return pl.BlockSpec(shape, lambda *_: (0,) * nd)


def _dot(a, b):
    return jnp.dot(a, b, preferred_element_type=f32)


def _dot_nt(a, b):
    return lax.dot_general(a, b, (((1,), (1,)), ((), ())), preferred_element_type=f32)


def _dot_tn(a, b):
    return lax.dot_general(a, b, (((0,), (0,)), ((), ())), preferred_element_type=f32)


def _rmsnorm(x, g):
    return x * lax.rsqrt(jnp.mean(x * x, axis=-1, keepdims=True) + EPS) * g


def _layernorm(x, g, b):
    mu = jnp.mean(x, axis=-1, keepdims=True)
    xc = x - mu
    return xc * lax.rsqrt(jnp.mean(xc * xc, axis=-1, keepdims=True) + EPS) * g + b


LOG2E = 1.4426950408889634
GELU_C = 0.7978845608028654
GELU_A = 0.044715


def _sigmoid(x):
    return 1.0 / (1.0 + jnp.exp2(x * (-LOG2E)))


def _silu(x):
    return x * _sigmoid(x)


def _gelu_tanh(x):
    k1 = -2.0 * GELU_C * LOG2E
    k2 = k1 * GELU_A
    return x / (1.0 + jnp.exp2(x * (k1 + k2 * (x * x))))


def _softplus(x):
    return jnp.maximum(x, 0.0) + jnp.log1p(jnp.exp(-jnp.abs(x)))


def _exact_parts(x, terms):
    parts = []
    r = x
    for _ in range(terms):
        p = r.astype(bf16).astype(f32)
        parts.append(p)
        r = r - p
    return parts


def _select_dot(x, sel, terms):
    m = x.shape[0]
    stacked = jnp.concatenate(_exact_parts(x, terms), axis=0).astype(bf16)
    r = _dot(stacked, sel)
    out = r[0:m]
    for i in range(1, terms):
        out = out + r[i * m:(i + 1) * m]
    return out


def _gated_group_norm(y, z, sn):
    yz = y * _silu(z)
    half = D_SSD // SSD_GROUPS
    outs = []
    for g in range(SSD_GROUPS):
        blk = yz[:, g * half:(g + 1) * half]
        outs.append(blk * lax.rsqrt(jnp.mean(blk * blk, axis=-1, keepdims=True) + EPS))
    return jnp.concatenate(outs, axis=1) * sn


def _in_proj(xn, wi_ref, wdt, lng, lnb):
    o1, o2, o3 = 2 * D_GMLP, 2 * D_GMLP + D_SSD, 2 * D_GMLP + D_SSD + CONV_DIM
    gu = _gelu_tanh(_dot(xn, wi_ref[:, 0:D_GMLP]))
    v = _layernorm(_gelu_tanh(_dot(xn, wi_ref[:, D_GMLP:o1])), lng, lnb)
    z = _dot(xn, wi_ref[:, o1:o2])
    xbc = _dot(xn, wi_ref[:, o2:o3])
    dtr = _dot(xn, wdt)
    return gu, v, z, xbc, dtr


def _p_mixer_body(x_ref, g_ref, wi_ref, wdt_ref, lng_ref, lnb_ref,
                  ws_ref, bsx_ref, cw_ref, cb_ref, dtb8_ref, a8_ref, dsk_ref, sn_ref,
                  triu_ref, esel_ref, wo_ref,
                  hp_ref, gv_ref, ssm_ref, conv_ref,
                  xp_scr, h_scr):
    j = pl.program_id(1)

    @pl.when(j == 0)
    def _():
        xp_scr[0:SUBLANES, :] = jnp.zeros((SUBLANES, CONV_DIM), f32)
        h_scr[...] = jnp.zeros_like(h_scr)

    t_idx = lax.broadcasted_iota(jnp.int32, (CHUNK, CHUNK), 0)
    s_idx = lax.broadcasted_iota(jnp.int32, (CHUNK, CHUNK), 1)
    causal = t_idx >= s_idx
    w_tril = [jnp.where(causal, ws_ref[hh], 0.0).astype(bf16) for hh in range(GMLP_HEADS)]
    gw = D_SSD // SSD_GROUPS
    hpg = SSD_HEADS // SSD_GROUPS
    zeros_pad = jnp.zeros((CHUNK - 3 * SSD_HEADS, CHUNK), f32)

    xn_all = _rmsnorm(x_ref[...], g_ref[...]).astype(bf16)
    o1, o2, o3 = 2 * D_GMLP, 2 * D_GMLP + D_SSD, 2 * D_GMLP + D_SSD + CONV_DIM

    def xn_rows(ci):
        return xn_all[ci * CHUNK:(ci + 1) * CHUNK, :]

    def out_project(ci, mix):
        rows = slice(ci * CHUNK, (ci + 1) * CHUNK)
        hp_ref[rows, :] = x_ref[rows, :] + _dot(mix, wo_ref[...])

    cur = _in_proj(xn_rows(0), wi_ref, wdt_ref[...], lng_ref[...], lnb_ref[...])
    prev_mix = None
    for ci in range(NCH):
        gu, v, z, xbc, dtr = cur
        nxt = ci + 1 < NCH

        xp_scr[SUBLANES:SUBLANES + CHUNK, :] = xbc
        acc = cb_ref[...]
        for k in range(CONV_W - 1):
            acc = acc + xp_scr[pl.ds(SUBLANES - (CONV_W - 1) + k, CHUNK), :] * cw_ref[k:k + 1, :]
        acc = acc + xbc * cw_ref[CONV_W - 1:CONV_W, :]
        xc = _silu(acc)
        tail = xbc[CHUNK - SUBLANES:, :]
        xp_scr[0:SUBLANES, :] = tail

        xs = xc[:, :D_SSD]
        b_all = xc[:, D_SSD:D_SSD + SSD_GROUPS * D_STATE].astype(bf16)
        c_all = xc[:, D_SSD + SSD_GROUPS * D_STATE:].astype(bf16)

        dt8 = _softplus(dtr.T[0:SSD_HEADS, :] + dtb8_ref[...])
        a8 = dt8 * a8_ref[...]
        a_parts = jnp.concatenate(_exact_parts(a8, 3) + [zeros_pad], axis=0).astype(bf16)
        a_sum = _dot(a_parts, triu_ref[...])
        if nxt:
            n_gu = _gelu_tanh(_dot(xn_rows(ci + 1), wi_ref[:, 0:D_GMLP]))
        acum8 = a_sum[0:8] + a_sum[8:16] + a_sum[16:24]
        a_last8 = acum8[:, CHUNK - 1:CHUNK]
        expa8 = jnp.exp(acum8)
        wgt8 = jnp.exp(a_last8 - acum8) * dt8
        cd8 = jnp.exp(a_last8)

        stacked = jnp.concatenate([dt8, expa8, wgt8, acum8], axis=0)
        lhs = jnp.concatenate(_exact_parts(stacked, 3) + [jnp.zeros((32, CHUNK), f32)], axis=0)
        bc = _dot(lhs.T.astype(bf16), esel_ref[...])
        dt_even = bc[:, 0:D_SSD]
        dt_odd = bc[:, D_SSD:2 * D_SSD]
        expa_e = bc[:, 2 * D_SSD:3 * D_SSD]
        wgt_e = bc[:, 3 * D_SSD:4 * D_SSD]
        col_b = bc[:, 4 * D_SSD:]

        cb = []
        for g in range(SSD_GROUPS):
            sl = slice(g * D_STATE, (g + 1) * D_STATE)
            cb.append(_dot_nt(c_all[:, sl], b_all[:, sl]))
        if nxt:
            n_v = _layernorm(_gelu_tanh(_dot(xn_rows(ci + 1), wi_ref[:, D_GMLP:o1])),
                             lng_ref[...], lnb_ref[...])

        vb = v.astype(bf16)
        mixed = jnp.concatenate(
            [_dot(w_tril[hh], vb[:, hh * GMLP_HD:(hh + 1) * GMLP_HD]) for hh in range(GMLP_HEADS)],
            axis=1) + bsx_ref[...]
        g_out = gu * mixed

        h_prev = h_scr[...].astype(bf16)
        y_off = jnp.concatenate(
            [_dot_nt(c_all[:, g * D_STATE:(g + 1) * D_STATE], h_prev[g * gw:(g + 1) * gw, :])
             for g in range(SSD_GROUPS)], axis=1) * expa_e
        if nxt:
            n_z = _dot(xn_rows(ci + 1), wi_ref[:, o1:o2])

        xw = (xs * wgt_e).astype(bf16)
        states = [_dot_tn(xw[:, g * gw:(g + 1) * gw], b_all[:, g * D_STATE:(g + 1) * D_STATE])
                  for g in range(SSD_GROUPS)]
        for h in range(SSD_HEADS):
            g, jh = h // hpg, h % hpg
            hr = slice(h * SSD_HEAD_DIM, (h + 1) * SSD_HEAD_DIM)
            h_scr[hr, :] = (h_scr[hr, :] * cd8[h:h + 1, :]
                            + states[g][jh * SSD_HEAD_DIM:(jh + 1) * SSD_HEAD_DIM, :])
        if nxt:
            n_xbc = _dot(xn_rows(ci + 1), wi_ref[:, o2:o3])
            n_dtr = _dot(xn_rows(ci + 1), wdt_ref[...])

        xdt_lo = xs * dt_even
        xdt_hi = xs * dt_odd
        y_diag = []
        for q in range(SSD_HEADS // 2):
            g = (2 * q) // hpg
            w_pair = []
            for h in (2 * q, 2 * q + 1):
                seg = col_b[:, h * CHUNK:(h + 1) * CHUNK] - acum8[h:h + 1, :]
                decay = jnp.exp(jnp.where(causal, seg, -jnp.inf))
                w_pair.append((cb[g] * decay).astype(bf16))
            sl = slice(q * LANES, (q + 1) * LANES)
            rhs = jnp.concatenate([xdt_lo[:, sl], xdt_hi[:, sl]], axis=0).astype(bf16)
            y_diag.append(_dot(jnp.concatenate(w_pair, axis=1), rhs))
        y_diag = jnp.concatenate(y_diag, axis=1)
        if prev_mix is not None:
            out_project(ci - 1, prev_mix)

        y = y_diag + y_off
        y = y + dsk_ref[...] * xs
        s_out = _gated_group_norm(y, z, sn_ref[...])
        prev_mix = jnp.concatenate([g_out, s_out], axis=1).astype(bf16)
        if nxt:
            cur = (n_gu, n_v, n_z, n_xbc, n_dtr)
    out_project(NCH - 1, prev_mix)

    gv_ref[0] = v
    conv_ref[0] = tail

    @pl.when(j == SEQ // TMIX - 1)
    def _():
        ssm_ref[0] = h_scr[...]


def _p_mixer(x2, g, wi, wdt, lng, lnb, ws, bsx, cw, cb, dtb8, a8, dsk, sn, triu, esel, wo):
    tiles = SEQ // TMIX
    row = lambda b, j: (b * tiles + j, 0)
    per_b = lambda b, j: (b, 0, 0)
    consts = (g, wi, wdt, lng, lnb, ws, bsx, cw, cb, dtb8, a8, dsk, sn, triu, esel, wo)
    return pl.pallas_call(
        _p_mixer_body,
        grid=(BATCH, tiles),
        in_specs=[pl.BlockSpec((TMIX, D_MODEL), row)] + [_const_spec(a.shape) for a in consts],
        out_specs=[
            pl.BlockSpec((TMIX, D_MODEL), row),
            pl.BlockSpec((1, CHUNK, D_GMLP), per_b),
            pl.BlockSpec((1, D_SSD, D_STATE), per_b),
            pl.BlockSpec((1, SUBLANES, CONV_DIM), per_b),
        ],
        out_shape=[
            jax.ShapeDtypeStruct((N_TOK, D_MODEL), f32),
            jax.ShapeDtypeStruct((BATCH, CHUNK, D_GMLP), f32),
            jax.ShapeDtypeStruct((BATCH, D_SSD, D_STATE), f32),
            jax.ShapeDtypeStruct((BATCH, SUBLANES, CONV_DIM), f32),
        ],
        scratch_shapes=[
            pltpu.VMEM((SUBLANES + CHUNK, CONV_DIM), f32),
            pltpu.VMEM((D_SSD, D_STATE), f32),
        ],
        compiler_params=_cparams(("arbitrary", "arbitrary")),
        name="p_mixer",
    )(x2, *consts)


def _memkv_body(m_ref, g_ref, wk_ref, wv_ref, k4_ref, v4_ref, kb_ref, vb_ref):
    mn = _rmsnorm(m_ref[...], g_ref[...]).astype(bf16)
    for w_ref, o4_ref, ob_ref in ((wk_ref, k4_ref, kb_ref), (wv_ref, v4_ref, vb_ref)):
        r = _dot(mn, w_ref[...])
        ob_ref[...] = r.astype(bf16)
        for h in range(X_HEADS):
            o4_ref[:, h, :] = r[:, h * X_HD:(h + 1) * X_HD]


def _memkv(mem2, g, wk, wv):
    n = mem2.shape[0]
    row = lambda i: (i, 0)
    row4 = lambda i: (i, 0, 0)
    spec4 = pl.BlockSpec((TM, X_HEADS, X_HD), row4)
    spec2 = pl.BlockSpec((TM, D_MODEL), row)
    shape4 = jax.ShapeDtypeStruct((n, X_HEADS, X_HD), f32)
    shape2 = jax.ShapeDtypeStruct((n, D_MODEL), bf16)
    return pl.pallas_call(
        _memkv_body,
        grid=(n // TM,),
        in_specs=[
            spec2,
            _const_spec((1, D_MODEL)),
            _const_spec((D_MODEL, D_MODEL)),
            _const_spec((D_MODEL, D_MODEL)),
        ],
        out_specs=[spec4, spec4, spec2, spec2],
        out_shape=[shape4, shape4, shape2, shape2],
        compiler_params=_cparams(("arbitrary",)),
        name="memkv",
    )(mem2, g, wk, wv)


def _p_attn_body(hp_ref, gx_ref, wq_ref, k_ref, v_ref, wxo_ref, out_ref):
    hp = hp_ref[...]
    hn = _rmsnorm(hp, gx_ref[...]).astype(bf16)
    q = _dot(hn, wq_ref[...]).astype(bf16)
    kb = k_ref[0]
    vb = v_ref[0]
    outs = []
    for h in range(X_HEADS):
        sl = slice(h * X_HD, (h + 1) * X_HD)
        s = _dot_nt(q[:, sl], kb[:, sl]) * (X_HD ** -0.5)
        e = jnp.exp(s - jnp.max(s, axis=-1, keepdims=True))
        p = e / jnp.sum(e, axis=-1, keepdims=True)
        outs.append(_dot(p.astype(bf16), vb[:, sl]))
    o = jnp.concatenate(outs, axis=1).astype(bf16)
    out_ref[...] = hp + _dot(o, wxo_ref[...])


def _p_attn(hp, gx, wq, mk, mv, wxo):
    tiles = SEQ // TM
    row = lambda b, j: (b * tiles + j, 0)
    per_b = lambda b, j: (b, 0, 0)
    return pl.pallas_call(
        _p_attn_body,
        grid=(BATCH, tiles),
        in_specs=[
            pl.BlockSpec((TM, D_MODEL), row),
            _const_spec((1, D_MODEL)),
            _const_spec((D_MODEL, D_MODEL)),
            pl.BlockSpec((1, N_MEM, D_MODEL), per_b),
            pl.BlockSpec((1, N_MEM, D_MODEL), per_b),
            _const_spec((D_MODEL, D_MODEL)),
        ],
        out_specs=pl.BlockSpec((TM, D_MODEL), row),
        out_shape=jax.ShapeDtypeStruct((N_TOK, D_MODEL), f32),
        compiler_params=_cparams(("arbitrary", "arbitrary")),
        name="p_attn",
    )(hp, gx, wq, mk, mv, wxo)


def _ffn(hp, gf, wg, wu, wd, gfin):
    hn = _rmsnorm(hp, gf).astype(bf16)
    hid = (_silu(_dot(hn, wg)) * _dot(hn, wu)).astype(bf16)
    return _rmsnorm(hp + _dot(hid, wd), gfin)


def _kv_copies(k_hbm, v_hbm, kbuf, vbuf, sem, step, slot):
    rows = pl.ds(step * SA, SA)
    out = []
    for h in range(X_HEADS):
        out.append(pltpu.make_async_copy(k_hbm.at[rows, :, h, :], kbuf.at[slot, h], sem.at[slot, 0, h]))
        out.append(pltpu.make_async_copy(v_hbm.at[rows, :, h, :], vbuf.at[slot, h], sem.at[slot, 1, h]))
    return out


def _p_ffn_body(h_ref, gf_ref, wg_ref, wu_ref, wd_ref, gfin_ref, q_ref, k_hbm, v_hbm,
                y_ref, o_ref, kbuf, vbuf, sem):
    step = pl.program_id(0)
    slot = step % 2
    copies = lambda s, sl: _kv_copies(k_hbm, v_hbm, kbuf, vbuf, sem, s, sl)

    @pl.when(step == 0)
    def _():
        for c in copies(0, 0):
            c.start()

    @pl.when(step + 1 < pl.num_programs(0))
    def _():
        for c in copies(step + 1, 1 - slot):
            c.start()

    for c in copies(step, slot):
        c.wait()

    y_ref[...] = _ffn(h_ref[...], gf_ref[...], wg_ref[...], wu_ref[...], wd_ref[...], gfin_ref[...])

    for i in range(SA):
        r = step * SA + i
        for h in range(X_HEADS):
            sl = slice(h * X_HD, (h + 1) * X_HD)
            q = q_ref[pl.ds(r, 1), sl]
            s = jnp.sum(kbuf[slot, h, i] * q, axis=-1, keepdims=True) * (X_HD ** -0.5)
            e = jnp.exp(s - jnp.max(s, axis=0, keepdims=True))
            p = e / jnp.sum(e, axis=0, keepdims=True)
            o_ref[pl.ds(r, 1), sl] = jnp.sum(p * vbuf[slot, h, i], axis=0, keepdims=True)


def _p_ffn(hp, gf, wg, wu, wd, gfin, q_s, ck, cv):
    row = lambda i: (i, 0)
    return pl.pallas_call(
        _p_ffn_body,
        grid=(N_TOK // TM,),
        in_specs=[
            pl.BlockSpec((TM, D_MODEL), row),
            _const_spec((1, D_MODEL)),
            _const_spec((D_MODEL, D_FF)),
            _const_spec((D_MODEL, D_FF)),
            _const_spec((D_FF, D_MODEL)),
            _const_spec((1, D_MODEL)),
            _const_spec((DEC_BATCH, D_MODEL)),
            pl.BlockSpec(memory_space=pl.ANY),
            pl.BlockSpec(memory_space=pl.ANY),
        ],
        out_specs=[pl.BlockSpec((TM, D_MODEL), row), _whole_spec((DEC_BATCH, D_MODEL))],
        out_shape=[jax.ShapeDtypeStruct((N_TOK, D_MODEL), f32),
                   jax.ShapeDtypeStruct((DEC_BATCH, D_MODEL), f32)],
        scratch_shapes=[
            pltpu.VMEM((2, X_HEADS, SA, N_MEM, X_HD), f32),
            pltpu.VMEM((2, X_HEADS, SA, N_MEM, X_HD), f32),
            pltpu.SemaphoreType.DMA((2, 2, X_HEADS)),
        ],
        compiler_params=_cparams(("arbitrary",)),
        name="p_ffn",
    )(hp, gf, wg, wu, wd, gfin, q_s, ck, cv)


def _s_inproj_body(x_ref, g_ref, wi_ref, wdt_ref, lng_ref, lnb_ref,
                   sc_ref, cw_ref, cb_ref, dtb_ref, a_ref, w00_ref, b0_ref, e64_ref,
                   gout_ref, v_ref, z_ref, conv_ref, xs_ref, xdt_ref, bc_ref, dec_ref):
    xn = _rmsnorm(x_ref[...], g_ref[...]).astype(bf16)
    gu, v, z, xbc, dtr = _in_proj(xn, wi_ref, wdt_ref[...], lng_ref[...], lnb_ref[...])
    v_ref[...] = v
    z_ref[...] = z
    gout_ref[...] = gu * (v * w00_ref[...] + b0_ref[...])

    acc = cb_ref[...]
    for k in range(CONV_W - 1):
        acc = acc + sc_ref[:, k * CONV_DIM:(k + 1) * CONV_DIM] * cw_ref[k:k + 1, :]
    acc = acc + xbc * cw_ref[CONV_W - 1:CONV_W, :]
    xc = _silu(acc)
    conv_ref[:, 0:(CONV_W - 2) * CONV_DIM] = sc_ref[:, CONV_DIM:(CONV_W - 1) * CONV_DIM]
    conv_ref[:, (CONV_W - 2) * CONV_DIM:] = xbc

    xs = xc[:, :D_SSD]
    dt = _softplus(dtr + dtb_ref[...])
    dec_ref[...] = jnp.exp(dt * a_ref[...])
    xs_ref[...] = xs
    xdt_ref[...] = xs * _select_dot(dt, e64_ref[...], 2)
    bc_ref[...] = xc[:, D_SSD:]


def _s_inproj(x, g, wi, wdt, lng, lnb, sc, cw, cb, dtb, a_neg, w00, b0, e64):
    nb = DEC_BATCH
    args = (x, g, wi, wdt, lng, lnb, sc, cw, cb, dtb, a_neg, w00, b0, e64)
    return pl.pallas_call(
        _s_inproj_body,
        grid=(1,),
        in_specs=[_const_spec(a.shape) for a in args],
        out_specs=[
            _whole_spec((nb, D_GMLP)), _whole_spec((nb, D_GMLP)), _whole_spec((nb, D_SSD)),
            _whole_spec((nb, (CONV_W - 1) * CONV_DIM)), _whole_spec((nb, D_SSD)),
            _whole_spec((nb, D_SSD)), _whole_spec((nb, 2 * SSD_GROUPS * D_STATE)),
            _whole_spec((nb, LANES)),
        ],
        out_shape=[
            jax.ShapeDtypeStruct((nb, D_GMLP), f32), jax.ShapeDtypeStruct((nb, D_GMLP), f32),
            jax.ShapeDtypeStruct((nb, D_SSD), f32),
            jax.ShapeDtypeStruct((nb, (CONV_W - 1) * CONV_DIM), f32),
            jax.ShapeDtypeStruct((nb, D_SSD), f32), jax.ShapeDtypeStruct((nb, D_SSD), f32),
            jax.ShapeDtypeStruct((nb, 2 * SSD_GROUPS * D_STATE), f32),
            jax.ShapeDtypeStruct((nb, LANES), f32),
        ],
        compiler_params=_cparams(("arbitrary",)),
        name="s_inproj",
    )(*args)


def _s_state_body(dec_ref, h0_ref, xdt_ref, bc_ref, h_ref, y_ref):
    step = pl.program_id(0)
    gw = D_SSD // SSD_GROUPS
    hpg = SSD_HEADS // SSD_GROUPS
    nbc = SSD_GROUPS * D_STATE
    xdt = xdt_ref[...]
    bc = bc_ref[...]
    row = lax.broadcasted_iota(jnp.int32, (SB, D_STATE), 0)
    y = [jnp.zeros((SB, gw), f32) for _ in range(SSD_GROUPS)]
    for i in range(SB):
        only_i = row == i
        for g in range(SSD_GROUPS):
            b_i = jnp.where(only_i, bc[:, g * D_STATE:(g + 1) * D_STATE], 0.0)
            c_i = jnp.where(only_i, bc[:, nbc + g * D_STATE:nbc + (g + 1) * D_STATE], 0.0)
            upd = _dot_tn(xdt[:, g * gw:(g + 1) * gw], b_i)
            blocks = []
            for j in range(hpg):
                h = g * hpg + j
                rows = slice(h * SSD_HEAD_DIM, (h + 1) * SSD_HEAD_DIM)
                d = dec_ref[(step * SB + i) * SSD_HEADS + h]
                blocks.append(h0_ref[i, rows, :] * d + upd[j * SSD_HEAD_DIM:(j + 1) * SSD_HEAD_DIM, :])
            h_new = jnp.concatenate(blocks, axis=0)
            h_ref[i, g * gw:(g + 1) * gw, :] = h_new
            y[g] = y[g] + _dot_nt(c_i.astype(bf16), h_new.astype(bf16))
    y_ref[...] = jnp.concatenate(y, axis=1)


def _s_state(dec, h0, xdt, bc):
    blk = lambda i: (i, 0, 0)
    row = lambda i: (i, 0)
    return pl.pallas_call(
        _s_state_body,
        grid=(DEC_BATCH // SB,),
        in_specs=[
            pl.BlockSpec(memory_space=pltpu.SMEM),
            pl.BlockSpec((SB, D_SSD, D_STATE), blk),
            pl.BlockSpec((SB, D_SSD), row),
            pl.BlockSpec((SB, 2 * SSD_GROUPS * D_STATE), row),
        ],
        out_specs=[pl.BlockSpec((SB, D_SSD, D_STATE), blk), pl.BlockSpec((SB, D_SSD), row)],
        out_shape=[jax.ShapeDtypeStruct((DEC_BATCH, D_SSD, D_STATE), f32),
                   jax.ShapeDtypeStruct((DEC_BATCH, D_SSD), f32)],
        compiler_params=_cparams(("arbitrary",)),
        name="s_state",
    )(dec, h0, xdt, bc)


def _s_mixout_body(x_ref, gout_ref, y_ref, xs_ref, z_ref, dsk_ref, sn_ref, wo_ref, gx_ref, wq_ref,
                   hs_ref, q_ref):
    y = y_ref[...] + dsk_ref[...] * xs_ref[...]
    s_out = _gated_group_norm(y, z_ref[...], sn_ref[...])
    mix = jnp.concatenate([gout_ref[...], s_out], axis=1).astype(bf16)
    hs = x_ref[...] + _dot(mix, wo_ref[...])
    hs_ref[...] = hs
    q_ref[...] = _dot(_rmsnorm(hs, gx_ref[...]).astype(bf16), wq_ref[...])


def _s_mixout(x, gout, y, xs, z, dsk, sn, wo, gx, wq):
    args = (x, gout, y, xs, z, dsk, sn, wo, gx, wq)
    return pl.pallas_call(
        _s_mixout_body,
        grid=(1,),
        in_specs=[_const_spec(a.shape) for a in args],
        out_specs=[_whole_spec((DEC_BATCH, D_MODEL)), _whole_spec((DEC_BATCH, D_MODEL))],
        out_shape=[jax.ShapeDtypeStruct((DEC_BATCH, D_MODEL), f32),
                   jax.ShapeDtypeStruct((DEC_BATCH, D_MODEL), f32)],
        compiler_params=_cparams(("arbitrary",)),
        name="s_mixout",
    )(*args)


def _s_ffn_body(hs_ref, o_ref, wxo_ref, gf_ref, wg_ref, wu_ref, wd_ref, gfin_ref, y_ref):
    hs = hs_ref[...] + _dot(o_ref[...].astype(bf16), wxo_ref[...])
    y_ref[...] = _ffn(hs, gf_ref[...], wg_ref[...], wu_ref[...], wd_ref[...], gfin_ref[...])


def _s_ffn(hs, o, wxo, gf, wg, wu, wd, gfin):
    args = (hs, o, wxo, gf, wg, wu, wd, gfin)
    return pl.pallas_call(
        _s_ffn_body,
        grid=(1,),
        in_specs=[_const_spec(a.shape) for a in args],
        out_specs=_whole_spec((DEC_BATCH, D_MODEL)),
        out_shape=jax.ShapeDtypeStruct((DEC_BATCH, D_MODEL), f32),
        compiler_params=_cparams(("arbitrary",)),
        name="s_ffn",
    )(*args)


def _triu_ones():
    return jnp.asarray(np.triu(np.ones((CHUNK, CHUNK), np.float32)), bf16)


def _chunk_select():
    e = np.zeros((LANES, 4 * D_SSD + SSD_HEADS * CHUNK), np.float32)
    for part in range(3):
        for h in range(SSD_HEADS):
            lanes = slice(h * SSD_HEAD_DIM, (h + 1) * SSD_HEAD_DIM)
            r = part * 32 + h
            e[r, (h % 2) * D_SSD:(h % 2 + 1) * D_SSD][lanes] = 1.0
            e[r + SSD_HEADS, 2 * D_SSD:3 * D_SSD][lanes] = 1.0
            e[r + 2 * SSD_HEADS, 3 * D_SSD:4 * D_SSD][lanes] = 1.0
            e[r + 3 * SSD_HEADS, 4 * D_SSD + h * CHUNK:4 * D_SSD + (h + 1) * CHUNK] = 1.0
    return jnp.asarray(e, bf16)


def _head_select(width):
    e = np.zeros((LANES, SSD_HEADS * width), np.float32)
    for h in range(SSD_HEADS):
        e[h, h * width:(h + 1) * width] = 1.0
    return jnp.asarray(e, bf16)


def _pad_lanes(v):
    return jnp.pad(v.astype(f32), (0, LANES - v.shape[0]))[None, :]


def kernel(x_prompt, x_sample, state_ssm, state_conv, cache_mem_k, cache_mem_v, mem_prompt,
           norm_mix, w_in, gmlp_ln_g, gmlp_ln_b, gmlp_ws, gmlp_bs, conv_w, conv_b, dt_bias,
           a_log, d_skip, ssd_norm, w_out, norm_xattn, norm_mem, w_q, w_k, w_v, w_xo,
           norm_ffn, w_gate, w_up, w_down, norm_final):
    l = 0
    wi = w_in[l].astype(bf16)
    o3 = 2 * D_GMLP + D_SSD + CONV_DIM
    wdt_in = jnp.pad(wi[:, o3:], ((0, 0), (0, LANES - SSD_HEADS)))
    wo = w_out[l].astype(bf16)
    wq, wk, wv, wxo = (w[l].astype(bf16) for w in (w_q, w_k, w_v, w_xo))
    wg, wu, wd = (w[l].astype(bf16) for w in (w_gate, w_up, w_down))

    g_mix, g_x, g_mem, g_ffn = (g[l][None, :] for g in (norm_mix, norm_xattn, norm_mem, norm_ffn))
    g_fin = norm_final[None, :]
    lng, lnb = gmlp_ln_g[l][None, :], gmlp_ln_b[l][None, :]
    cw, cb = conv_w[l], conv_b[l][None, :]
    a_vec = -jnp.exp(a_log[l].astype(f32))
    dtb = _pad_lanes(dt_bias[l])
    a_neg = _pad_lanes(a_vec)
    dtb8 = jnp.broadcast_to(dt_bias[l].astype(f32)[:, None], (SSD_HEADS, CHUNK))
    a8 = jnp.broadcast_to(a_vec[:, None], (SSD_HEADS, CHUNK))
    dsk = jnp.repeat(d_skip[l].astype(f32), SSD_HEAD_DIM)[None, :]
    sn = ssd_norm[l][None, :]
    ws = gmlp_ws[l]
    bsx = jnp.repeat(jnp.swapaxes(gmlp_bs[l], 0, 1), GMLP_HD, axis=1)
    w00 = jnp.repeat(ws[:, 0, 0], GMLP_HD)[None, :]
    b0 = bsx[0:1, :]
    e64 = _head_select(SSD_HEAD_DIM)

    x2 = x_prompt.reshape(N_TOK, D_MODEL)
    hp1, gv_p, ssm_p, conv8 = _p_mixer(x2, g_mix, wi, wdt_in, lng, lnb, ws, bsx, cw, cb, dtb8, a8,
                                       dsk, sn, _triu_ones(), _chunk_select(), wo)
    mk, mv, mkb, mvb = _memkv(mem_prompt.reshape(BATCH * N_MEM, D_MODEL), g_mem, wk, wv)
    hp2 = _p_attn(hp1, g_x, wq, mkb.reshape(BATCH, N_MEM, D_MODEL),
                  mvb.reshape(BATCH, N_MEM, D_MODEL), wxo)

    xs2 = x_sample.reshape(DEC_BATCH, D_MODEL)
    sc = state_conv[l].reshape(DEC_BATCH, (CONV_W - 1) * CONV_DIM)
    gout, v_s, z_s, conv_s, xs_s, xdt_s, bc_s, dec_s = _s_inproj(
        xs2, g_mix, wi, wdt_in, lng, lnb, sc, cw, cb, dtb, a_neg, w00, b0, e64)
    h0 = state_ssm[l].reshape(DEC_BATCH, D_SSD, D_STATE)
    ssm_s, y_s = _s_state(dec_s[:, :SSD_HEADS].reshape(-1), h0, xdt_s, bc_s)
    hs, q_s = _s_mixout(xs2, gout, y_s, xs_s, z_s, dsk, sn, wo, g_x, wq)

    y_prompt, o_s = _p_ffn(hp2, g_ffn, wg, wu, wd, g_fin, q_s, cache_mem_k[l], cache_mem_v[l])
    y_prompt = y_prompt.reshape(BATCH, SEQ, D_MODEL)
    y_sample = _s_ffn(hs, o_s, wxo, g_ffn, wg, wu, wd, g_fin).reshape(DEC_BATCH, 1, D_MODEL)

    return (
        y_prompt,
        y_sample,
        ssm_p.reshape(1, BATCH, SSD_HEADS, SSD_HEAD_DIM, D_STATE),
        conv8[:, SUBLANES - (CONV_W - 1):, :][None],
        gv_p[None],
        mk.reshape(1, BATCH, N_MEM, X_HEADS, X_HD),
        mv.reshape(1, BATCH, N_MEM, X_HEADS, X_HD),
        ssm_s.reshape(1, DEC_BATCH, SSD_HEADS, SSD_HEAD_DIM, D_STATE),
        conv_s.reshape(1, DEC_BATCH, CONV_W - 1, CONV_DIM),
        v_s.reshape(1, DEC_BATCH, 1, D_GMLP),
    )
```
